```python
import math
import jax, jax.numpy as jnp
from jax import lax
import numpy as np

D_MODEL = 2048
BATCH = 4
SEQ = 2048
DEPTH = 1
DEC_BATCH = 128
DEC_SEQ = 8
PAST_LEN = 2048
PAGE_SIZE = 128

F32 = jnp.float32
N_META = 16
RMS_EPS = 1e-6
M_H = 4
M_DK = 128
M_DV = 256
M_CHUNK = 64
F_BIAS_INIT = 3.0
A_H = 8
A_DH = 128
IDX_H = 16
IDX_D = 64
TOPK_MAX = 256
Q_BLOCK = 128
REL_BUCKETS = 32
REL_MAX_DIST = 128
PEER_H = 8
PEER_DQ = 256
N_KEYS = 128
N_EXPERTS = N_KEYS * N_KEYS
PEER_TOPK = 16
PEER_BLOCK = 128
IN_SPLITS = (M_H * M_DK, M_H * M_DK, M_H * M_DV, M_H * M_DV, M_H, M_H,
             A_H * A_DH, A_H * A_DH, A_H * A_DH, IDX_H * IDX_D, IDX_D, IDX_H)
P_IN = sum(IN_SPLITS)
MIX = M_H * M_DV + A_H * A_DH

kernel_name = 'mlstm_dsa_peer_hybrid_step'


def rmsnorm(x, g):
    xf = x.astype(F32)
    y = xf * lax.rsqrt(jnp.mean(xf * xf, axis=-1, keepdims=True) + RMS_EPS)
    return (y * g.astype(F32)).astype(x.dtype)


def mixer_inputs(hn, w_in, b_i, b_f):
    B, T, _ = hn.shape
    cuts = np.cumsum(IN_SPLITS)[:-1].tolist()
    mq, mk, mv, mo, ig, fg, aq, ak, av, iq, ik, iw = jnp.split(hn @ w_in, cuts, axis=-1)

    def heads(a, d):
        return a.reshape(B, T, M_H, d).transpose(0, 2, 1, 3).astype(F32)

    li = (ig + b_i).astype(F32).transpose(0, 2, 1)
    lf = jax.nn.log_sigmoid((fg + b_f).astype(F32)).transpose(0, 2, 1)
    m_in = (heads(mq, M_DK), heads(mk, M_DK) * (M_DK ** -0.5), heads(mv, M_DV), li, lf)
    dsa_in = (aq.reshape(B, T, A_H, A_DH), ak.reshape(B, T, A_H, A_DH), av.reshape(B, T, A_H, A_DH),
              iq.reshape(B, T, IDX_H, IDX_D), ik, iw)
    return m_in, mo, dsa_in


def mlstm_chunk(carry, inp):
    C, n, m = carry
    q, k, v, li, lf = inp
    L = q.shape[2]
    b = jnp.cumsum(lf, axis=-1)
    causal = jnp.tril(jnp.ones((L, L), dtype=bool))
    dmat = jnp.where(causal, b[..., :, None] - b[..., None, :] + li[..., None, :], -jnp.inf)
    inter = b + m[..., None]
    m_t = jnp.maximum(inter, jnp.max(dmat, axis=-1))
    w_inter = jnp.exp(inter - m_t)
    s = jnp.einsum('bhtd,bhsd->bhts', q, k) * jnp.exp(dmat - m_t[..., None])
    num = w_inter[..., None] * jnp.einsum('bhtd,bhde->bhte', q, C) + jnp.einsum('bhts,bhse->bhte', s, v)
    den = w_inter * jnp.einsum('bhtd,bhd->bht', q, n) + jnp.sum(s, axis=-1)
    h = num / jnp.maximum(jnp.abs(den), jnp.exp(-m_t))[..., None]
    b_last = b[..., -1]
    g = b_last[..., None] - b + li
    m_new = jnp.maximum(b_last + m, jnp.max(g, axis=-1))
    a = jnp.exp(b_last + m - m_new)
    wk = jnp.exp(g - m_new[..., None])
    C_new = a[..., None, None] * C + jnp.einsum('bhs,bhsd,bhse->bhde', wk, k, v)
    n_new = a[..., None] * n + jnp.einsum('bhs,bhsd->bhd', wk, k)
    return (C_new, n_new, m_new), h


def mlstm_scan(carry, inp):
    T = inp[0].shape[2]
    nc = T // M_CHUNK

    def to_chunks(a):
        return jnp.moveaxis(a.reshape(a.shape[:2] + (nc, M_CHUNK) + a.shape[3:]), 2, 0)

    carry, hs = lax.scan(mlstm_chunk, carry, tuple(to_chunks(a) for a in inp))
    h = jnp.moveaxis(hs, 0, 2)
    return carry, h.reshape(h.shape[:2] + (T, M_DV))


def t5_bias(rel, table):
    max_exact = REL_BUCKETS // 2
    nf = jnp.maximum(rel, 1).astype(F32)
    large = max_exact + (jnp.log(nf / max_exact) / math.log(REL_MAX_DIST / max_exact)
                         * (REL_BUCKETS - max_exact)).astype(jnp.int32)
    large = jnp.minimum(large, REL_BUCKETS - 1)
    bucket = jnp.where(rel < max_exact, rel, large)
    return table[bucket]


def dsa_attend(q, qi, wi, q_pos, k_all, v_all, ki_all, rel_table, topk):
    S = k_all.shape[1]
    key_pos = jnp.arange(S, dtype=jnp.int32)
    dots = jax.nn.relu(jnp.einsum('bthd,bsd->bths', qi.astype(F32), ki_all.astype(F32)))
    score = jnp.einsum('bth,bths->bts', wi.astype(F32) * (IDX_D ** -0.5 * IDX_H ** -0.5), dots)
    causal = key_pos[None, :] <= q_pos[:, None]
    score = jnp.where(causal[None], score, -jnp.inf)
    _, idx = lax.top_k(score, topk)
    valid = idx <= q_pos[None, :, None]
    gather = jax.vmap(lambda rows, ids: rows[ids])
    k_sel = gather(k_all, idx).astype(F32)
    v_sel = gather(v_all, idx).astype(F32)
    logits = jnp.einsum('bthd,btjhd->bhtj', q.astype(F32), k_sel) * (A_DH ** -0.5)
    bias = t5_bias(jnp.maximum(q_pos[None, :, None] - idx, 0), rel_table.astype(F32))
    logits = jnp.where(valid[:, None], logits + bias.transpose(0, 3, 1, 2), -jnp.inf)
    p = jax.nn.softmax(logits, axis=-1)
    return jnp.einsum('bhtj,btjhd->bthd', p, v_sel).astype(q.dtype)


def dsa_prompt_blocks(q, qi, wi, q_pos, k_all, v_all, ki_all, rel_table, topk):
    B, Tq = q.shape[:2]
    nb = Tq // Q_BLOCK

    def blk(a):
        return jnp.moveaxis(a.reshape((B, nb, Q_BLOCK) + a.shape[2:]), 1, 0)

    def one(args):
        qb, qib, wib, pb = args
        return dsa_attend(qb, qib, wib, pb, k_all, v_all, ki_all, rel_table, topk)

    out = lax.map(one, (blk(q), blk(qi), blk(wi), q_pos.reshape(nb, Q_BLOCK)))
    return jnp.moveaxis(out, 0, 1).reshape(q.shape)


def mixer_output(h_m, mo, attn, mh_g, w_out):
    B, H, T, DV = h_m.shape
    hm = h_m * lax.rsqrt(jnp.mean(h_m * h_m, axis=-1, keepdims=True) + RMS_EPS)
    hm = hm.transpose(0, 2, 1, 3).reshape(B, T, H * DV) * mh_g.astype(F32) * jax.nn.sigmoid(mo.astype(F32))
    mixed = jnp.concatenate([hm.astype(mo.dtype), attn.reshape(B, T, A_H * A_DH).astype(mo.dtype)], axis=-1)
    return mixed @ w_out


def peer_block(x, wq, k1, k2, U, V):
    N = x.shape[0]
    qq = (x @ wq).reshape(N, PEER_H, 2, PEER_DQ // 2).astype(F32)
    s1 = jnp.einsum('nhd,hkd->nhk', qq[:, :, 0], k1.astype(F32))
    s2 = jnp.einsum('nhd,hkd->nhk', qq[:, :, 1], k2.astype(F32))
    v1, i1 = lax.top_k(s1, PEER_TOPK)
    v2, i2 = lax.top_k(s2, PEER_TOPK)
    cand = (v1[..., :, None] + v2[..., None, :]).reshape(N, PEER_H, PEER_TOPK * PEER_TOPK)
    sc, ci = lax.top_k(cand, PEER_TOPK)
    e = (jnp.take_along_axis(i1, ci // PEER_TOPK, axis=-1) * N_KEYS
         + jnp.take_along_axis(i2, ci % PEER_TOPK, axis=-1))
    g = jax.nn.softmax(sc, axis=-1)
    act = jax.nn.gelu(jnp.einsum('nd,nhjd->nhj', x.astype(F32), U[e].astype(F32)), approximate=False)
    out = jnp.einsum('nhj,nhjd->nd', g * act, V[e].astype(F32))
    return out.astype(x.dtype)


def peer(x, wq, k1, k2, U, V):
    B, T, D = x.shape
    n = B * T
    pad = (-n) % PEER_BLOCK
    blocks = jnp.pad(x.reshape(n, D), ((0, pad), (0, 0))).reshape(-1, PEER_BLOCK, D)
    out = lax.map(lambda xb: peer_block(xb, wq, k1, k2, U, V), blocks)
    return out.reshape(-1, D)[:n].reshape(B, T, D)


def paged_rows(pool, page_table):
    g = pool[page_table]
    return g.reshape((g.shape[0], g.shape[1] * g.shape[2]) + g.shape[3:])


def setup_inputs(seed: int = 0) -> dict:
    key = jax.random.key(seed)
    ks = jax.random.split(key, 24)
    n_pages = PAST_LEN // PAGE_SIZE
    n_used = DEC_BATCH * n_pages
    n_pool = n_used + max(1, n_used // 4)

    def nrm(k, shape, s):
        return jax.random.normal(k, shape, F32) * s

    page_table = jax.random.permutation(ks[8], n_pool)[:n_used].reshape(DEC_BATCH, n_pages).astype(jnp.int32)
    return {
        'x_prompt': nrm(ks[0], (BATCH, SEQ, D_MODEL), 1.0),
        'x_sample': nrm(ks[1], (DEC_BATCH, DEC_SEQ, D_MODEL), 1.0),
        'cache_k': nrm(ks[2], (DEPTH, n_pool, PAGE_SIZE, A_H, A_DH), 1.0),
        'cache_v': nrm(ks[3], (DEPTH, n_pool, PAGE_SIZE, A_H, A_DH), 1.0),
        'cache_kidx': nrm(ks[4], (DEPTH, n_pool, PAGE_SIZE, IDX_D), 1.0),
        'state_C': nrm(ks[5], (DEPTH, DEC_BATCH, M_H, M_DK, M_DV), 0.1),
        'state_n': nrm(ks[6], (DEPTH, DEC_BATCH, M_H, M_DK), 0.1),
        'state_m': nrm(ks[7], (DEPTH, DEC_BATCH, M_H), 1.0),
        'page_table': page_table,
        'meta_tokens': nrm(ks[9], (N_META, D_MODEL), 1.0),
        'rel_bias': nrm(ks[10], (REL_BUCKETS, A_H), 0.5),
        'final_norm_g': 1.0 + nrm(ks[11], (D_MODEL,), 0.02),
        'norm1_g': 1.0 + nrm(ks[12], (DEPTH, D_MODEL), 0.02),
        'w_in': nrm(ks[13], (DEPTH, D_MODEL, P_IN), D_MODEL ** -0.5),
        'b_i': nrm(ks[14], (DEPTH, M_H), 0.1),
        'b_f': F_BIAS_INIT + nrm(ks[15], (DEPTH, M_H), 0.1),
        'mh_norm_g': 1.0 + nrm(ks[16], (DEPTH, M_H * M_DV), 0.02),
        'w_out': nrm(ks[17], (DEPTH, MIX, D_MODEL), MIX ** -0.5),
        'norm2_g': 1.0 + nrm(ks[18], (DEPTH, D_MODEL), 0.02),
        'peer_wq': nrm(ks[19], (DEPTH, D_MODEL, PEER_H * PEER_DQ), D_MODEL ** -0.5),
        'peer_k1': nrm(ks[20], (DEPTH, PEER_H, N_KEYS, PEER_DQ // 2), (PEER_DQ // 2) ** -0.5),
        'peer_k2': nrm(ks[21], (DEPTH, PEER_H, N_KEYS, PEER_DQ // 2), (PEER_DQ // 2) ** -0.5),
        'peer_u': nrm(ks[22], (DEPTH, N_EXPERTS, D_MODEL), D_MODEL ** -0.5),
        'peer_v': nrm(ks[23], (DEPTH, N_EXPERTS, D_MODEL), PEER_H ** -0.5),
    }


def reference(x_prompt, x_sample, cache_k, cache_v, cache_kidx, state_C, state_n, state_m, page_table,
              meta_tokens, rel_bias, final_norm_g, norm1_g, w_in, b_i, b_f, mh_norm_g, w_out, norm2_g,
              peer_wq, peer_k1, peer_k2, peer_u, peer_v):
    B, S_p, _ = x_prompt.shape
    Bs, Ts, _ = x_sample.shape
    past = page_table.shape[1] * PAGE_SIZE
    topk_p = min(TOPK_MAX, S_p // 4)
    topk_s = min(TOPK_MAX, (past + Ts) // 4)
    T = N_META + S_p
    pos_p = jnp.arange(T, dtype=jnp.int32)
    pos_s = past + jnp.arange(Ts, dtype=jnp.int32)
    meta = jnp.broadcast_to(meta_tokens.astype(x_prompt.dtype)[None], (B, N_META, D_MODEL))
    hp = jnp.concatenate([meta, x_prompt], axis=1)
    hs = x_sample
    kp, vp, ip, cp, np_, mp = [], [], [], [], [], []
    ks_, vs_, is_, cs_, ns_, ms_ = [], [], [], [], [], []
    for l in range(DEPTH):
        m_in, mo, (aq, ak, av, iq, ik, iw) = mixer_inputs(rmsnorm(hp, norm1_g[l]), w_in[l], b_i[l], b_f[l])
        carry = (jnp.zeros((B, M_H, M_DK, M_DV), F32), jnp.zeros((B, M_H, M_DK), F32), jnp.zeros((B, M_H), F32))
        carry, hm_meta = mlstm_chunk(carry, tuple(a[:, :, :N_META] for a in m_in))
        carry, hm_real = mlstm_scan(carry, tuple(a[:, :, N_META:] for a in m_in))
        attn_real = dsa_prompt_blocks(aq[:, N_META:], iq[:, N_META:], iw[:, N_META:], pos_p[N_META:],
                                      ak, av, ik, rel_bias, topk_p)
        if l == DEPTH - 1:
            hp = hp[:, N_META:] + mixer_output(hm_real, mo[:, N_META:], attn_real, mh_norm_g[l], w_out[l])
        else:
            attn_meta = dsa_attend(aq[:, :N_META], iq[:, :N_META], iw[:, :N_META], pos_p[:N_META],
                                   ak[:, :N_META], av[:, :N_META], ik[:, :N_META], rel_bias, N_META)
            hp = hp + mixer_output(jnp.concatenate([hm_meta, hm_real], axis=2), mo,
                                   jnp.concatenate([attn_meta, attn_real], axis=1), mh_norm_g[l], w_out[l])
        hp = hp + peer(rmsnorm(hp, norm2_g[l]), peer_wq[l], peer_k1[l], peer_k2[l], peer_u[l], peer_v[l])
        kp.append(ak)
        vp.append(av)
        ip.append(ik)
        cp.append(carry[0])
        np_.append(carry[1])
        mp.append(carry[2])
        m_in_s, mo_s, (aq_s, ak_s, av_s, iq_s, ik_s, iw_s) = mixer_inputs(rmsnorm(hs, norm1_g[l]), w_in[l], b_i[l], b_f[l])
        carry_s = (state_C[l].astype(F32), state_n[l].astype(F32), state_m[l].astype(F32))
        carry_s, hm_s = mlstm_chunk(carry_s, m_in_s)
        k_all = jnp.concatenate([paged_rows(cache_k[l], page_table).astype(ak_s.dtype), ak_s], axis=1)
        v_all = jnp.concatenate([paged_rows(cache_v[l], page_table).astype(av_s.dtype), av_s], axis=1)
        ki_all = jnp.concatenate([paged_rows(cache_kidx[l], page_table).astype(ik_s.dtype), ik_s], axis=1)
        attn_s = dsa_attend(aq_s, iq_s, iw_s, pos_s, k_all, v_all, ki_all, rel_bias, topk_s)
        hs = hs + mixer_output(hm_s, mo_s, attn_s, mh_norm_g[l], w_out[l])
        hs = hs + peer(rmsnorm(hs, norm2_g[l]), peer_wq[l], peer_k1[l], peer_k2[l], peer_u[l], peer_v[l])
        ks_.append(ak_s)
        vs_.append(av_s)
        is_.append(ik_s)
        cs_.append(carry_s[0])
        ns_.append(carry_s[1])
        ms_.append(carry_s[2])
    y_prompt = rmsnorm(hp, final_norm_g)
    y_sample = rmsnorm(hs, final_norm_g)
    return (y_prompt, y_sample,
            jnp.stack(kp), jnp.stack(vp), jnp.stack(ip), jnp.stack(cp), jnp.stack(np_), jnp.stack(mp),
            jnp.stack(ks_), jnp.stack(vs_), jnp.stack(is_), jnp.stack(cs_), jnp.stack(ns_), jnp.stack(ms_))
```

```python
import functools
import math

import numpy as np
import jax
import jax.numpy as jnp
from jax import lax
from jax.experimental import pallas as pl
from jax.experimental.pallas import tpu as pltpu

F32 = jnp.float32
BF16 = jnp.bfloat16
I32 = jnp.int32

D_MODEL = 2048
N_META = 16
RMS_EPS = 1e-6
M_H, M_DK, M_DV = 4, 128, 256
A_H, A_DH = 8, 128
IDX_H, IDX_D = 16, 64
TOPK_MAX = 256
REL_BUCKETS, REL_MAX_DIST = 32, 128
PEER_H, PEER_DQ, N_KEYS, PEER_TOPK = 8, 256, 128, 16
N_EXPERTS = N_KEYS * N_KEYS
PAGE = 128
IN_SPLITS = (M_H * M_DK, M_H * M_DK, M_H * M_DV, M_H * M_DV, M_H, M_H,
             A_H * A_DH, A_H * A_DH, A_H * A_DH, IDX_H * IDX_D, IDX_D, IDX_H)

LANE = 128
VMEM_LIMIT = 56 * 1024 * 1024

COL_AQ, COL_AK, COL_AV, COL_IQ, COL_MV, COL_MO, COL_MQ, COL_MK, COL_SM = (
    0, 1024, 2048, 3072, 4096, 5120, 6144, 6656, 7168)
P_PAD = COL_SM + LANE
SM_IK, SM_IW, SM_IG, SM_FG = 0, 64, 80, 84
T_PAD = 17 * LANE
META_TILE = 16
META_OFF = LANE - N_META
INT_MIN = -(2 ** 31)
NEG_BIG = -1e30


def _cparams(sem, vmem=VMEM_LIMIT):
    return pltpu.CompilerParams(dimension_semantics=sem, vmem_limit_bytes=vmem)


def _resident(shape, index_map):
    return pl.BlockSpec(shape, index_map, pipeline_mode=pl.Buffered(1))


def _dot(a, b):
    return jnp.dot(a, b, preferred_element_type=F32)


def _dot_nt(a, b):
    return lax.dot_general(a, b, (((1,), (1,)), ((), ())), preferred_element_type=F32)


def _dot_tn(a, b):
    return lax.dot_general(a, b, (((0,), (0,)), ((), ())), preferred_element_type=F32)


IP_TM, IP_TN = 512, 384


def _inproj_kernel(x_ref, g_ref, w_ref, of_ref, ob_ref, hn_ref):
    @pl.when(pl.program_id(1) == 0)
    def _():
        x = x_ref[...]
        y = x * lax.rsqrt(jnp.mean(x * x, axis=-1, keepdims=True) + RMS_EPS) * g_ref[...]
        hn_ref[...] = y.astype(BF16)

    acc = _dot(hn_ref[...], w_ref[...])
    of_ref[...] = acc
    ob_ref[...] = acc.astype(BF16)


def _inproj(x2d, g, w):
    rows = x2d.shape[0]
    grid = (rows // IP_TM, P_PAD // IP_TN)
    return pl.pallas_call(
        _inproj_kernel,
        grid=grid,
        in_specs=[pl.BlockSpec((IP_TM, D_MODEL), lambda i, j: (i, 0)),
                  pl.BlockSpec((1, D_MODEL), lambda i, j: (0, 0)),
                  pl.BlockSpec((D_MODEL, IP_TN), lambda i, j: (0, j))],
        out_specs=[pl.BlockSpec((IP_TM, IP_TN), lambda i, j: (i, j)),
                   pl.BlockSpec((IP_TM, IP_TN), lambda i, j: (i, j))],
        out_shape=[jax.ShapeDtypeStruct((rows, P_PAD), F32),
                   jax.ShapeDtypeStruct((rows, P_PAD), BF16)],
        scratch_shapes=[pltpu.VMEM((IP_TM, D_MODEL), BF16)],
        compiler_params=_cparams(("arbitrary", "arbitrary")),
        name="inproj",
    )(x2d, g, w)


def _mlstm_gates(gates, bias_row, lo, hi):
    row = lax.broadcasted_iota(I32, (LANE, LANE), 0)
    col = lax.broadcasted_iota(I32, (LANE, LANE), 1)
    valid = (row >= lo) & (row < hi)
    pre = gates + bias_row
    li = jnp.where(valid, pre, -jnp.inf)
    lf = jnp.where(valid, jax.nn.log_sigmoid(pre), 0.0)
    tril = jnp.where(col <= row, 1.0, 0.0).astype(F32)
    b = jnp.dot(tril, lf, preferred_element_type=F32, precision=lax.Precision.HIGHEST)
    return li, b, li.T, b.T


def _mlstm_head(q, k, v, li, b, lit, bt, h, c_old, n_old, m_old):
    li_col = li[:, SM_IG + h:SM_IG + h + 1]
    b_col = b[:, SM_FG + h:SM_FG + h + 1]
    li_row = lit[SM_IG + h:SM_IG + h + 1, :]
    b_row = bt[SM_FG + h:SM_FG + h + 1, :]
    row = lax.broadcasted_iota(I32, (LANE, LANE), 0)
    col = lax.broadcasted_iota(I32, (LANE, LANE), 1)
    dmat = jnp.where(col <= row, b_col - b_row + li_row, -jnp.inf)
    inter = b_col + m_old
    m_t = jnp.maximum(inter, jnp.max(dmat, axis=-1, keepdims=True))
    w_inter = jnp.exp(inter - m_t)
    scale = M_DK ** -0.5
    s = _dot_nt(q, k) * scale * jnp.exp(dmat - m_t)
    num = w_inter * _dot(q, c_old.astype(BF16)) + _dot(s.astype(BF16), v)
    qf = q.astype(F32)
    den = w_inter * jnp.sum(qf * n_old, axis=-1, keepdims=True) + jnp.sum(s, axis=-1, keepdims=True)
    h_out = num / jnp.maximum(jnp.abs(den), jnp.exp(-m_t))
    b_last = b_col[LANE - 1:LANE, :]
    g_col = b_last - b_col + li_col
    m_new = jnp.maximum(b_last + m_old, jnp.max(g_col, axis=0, keepdims=True))
    a = jnp.exp(b_last + m_old - m_new)
    wk = jnp.exp(g_col - m_new)
    wv = (wk * v.astype(F32)).astype(BF16)
    c_new = a * c_old + scale * _dot_tn(k, wv)
    n_new = a * n_old + scale * jnp.sum(wk * k.astype(F32), axis=0, keepdims=True)
    return h_out, c_new, n_new, m_new


def _mlstm_mix(h_out, mo, g):
    hn = h_out * lax.rsqrt(jnp.mean(h_out * h_out, axis=-1, keepdims=True) + RMS_EPS)
    return hn * g * jax.nn.sigmoid(mo)


def _mlstm_prompt_kernel(q_ref, k_ref, v_ref, mo_ref, g_ref, bias_ref, mhg_ref,
                         out_ref, c_out, n_out, m_out, c_scr, n_scr, m_scr):
    c = pl.program_id(1)

    @pl.when(c == 0)
    def _():
        c_scr[...] = jnp.zeros_like(c_scr)
        n_scr[...] = jnp.zeros_like(n_scr)
        m_scr[...] = jnp.zeros_like(m_scr)

    lo = jnp.where(c == 0, META_OFF, 0)
    li, b, lit, bt = _mlstm_gates(g_ref[...], bias_ref[...], lo, LANE)
    for h in range(M_H):
        q = q_ref[:, h * M_DK:(h + 1) * M_DK]
        k = k_ref[:, h * M_DK:(h + 1) * M_DK]
        v = v_ref[:, h * M_DV:(h + 1) * M_DV]
        h_out, c_new, n_new, m_new = _mlstm_head(
            q, k, v, li, b, lit, bt, h, c_scr[h], n_scr[h], m_scr[h][:, 0:1])
        c_scr[h] = c_new
        n_scr[h] = n_new
        m_scr[h] = jnp.broadcast_to(m_new, (1, LANE))
        sl = slice(h * M_DV, (h + 1) * M_DV)
        out_ref[:, sl] = _mlstm_mix(h_out, mo_ref[:, sl], mhg_ref[:, sl]).astype(out_ref.dtype)

    @pl.when(c == pl.num_programs(1) - 1)
    def _():
        c_out[...] = c_scr[...]
        n_out[...] = n_scr[...]
        m_out[...] = m_scr[...]


def _mlstm_prompt(slab_b, slab_f, bias_row, mhg):
    nb = slab_b.shape[0]
    nchunk = T_PAD // LANE

    def tile(c):
        return (c + META_TILE) % nchunk

    return pl.pallas_call(
        _mlstm_prompt_kernel,
        grid=(nb, nchunk),
        in_specs=[pl.BlockSpec((None, LANE, M_H * M_DK), lambda b, c: (b, tile(c), COL_MQ // 512)),
                  pl.BlockSpec((None, LANE, M_H * M_DK), lambda b, c: (b, tile(c), COL_MK // 512)),
                  pl.BlockSpec((None, LANE, M_H * M_DV), lambda b, c: (b, tile(c), COL_MV // 1024)),
                  pl.BlockSpec((None, LANE, M_H * M_DV), lambda b, c: (b, tile(c), COL_MO // 1024)),
                  pl.BlockSpec((None, LANE, LANE), lambda b, c: (b, tile(c), COL_SM // LANE)),
                  pl.BlockSpec((1, LANE), lambda b, c: (0, 0)),
                  pl.BlockSpec((1, M_H * M_DV), lambda b, c: (0, 0))],
        out_specs=[pl.BlockSpec((None, LANE, M_H * M_DV), lambda b, c: (b, jnp.maximum(c - 1, 0), 0)),
                   pl.BlockSpec((None, M_H, M_DK, M_DV), lambda b, c: (b, 0, 0, 0)),
                   pl.BlockSpec((None, M_H, 1, M_DK), lambda b, c: (b, 0, 0, 0)),
                   pl.BlockSpec((None, M_H, 1, LANE), lambda b, c: (b, 0, 0, 0))],
        out_shape=[jax.ShapeDtypeStruct((nb, T_PAD - LANE, M_H * M_DV), BF16),
                   jax.ShapeDtypeStruct((nb, M_H, M_DK, M_DV), F32),
                   jax.ShapeDtypeStruct((nb, M_H, 1, M_DK), F32),
                   jax.ShapeDtypeStruct((nb, M_H, 1, LANE), F32)],
        scratch_shapes=[pltpu.VMEM((M_H, M_DK, M_DV), F32),
                        pltpu.VMEM((M_H, 1, M_DK), F32),
                        pltpu.VMEM((M_H, 1, LANE), F32)],
        compiler_params=_cparams(("arbitrary", "arbitrary")),
        name="mlstm_prompt",
    )(slab_b, slab_b, slab_b, slab_f, slab_f, bias_row, mhg)


MS_NB = 2


def _mlstm_sample_kernel(q_ref, k_ref, v_ref, mo_ref, g_ref, bias_ref, mhg_ref, c_in, n_in, m_in,
                         out_ref, c_out, n_out, m_out, qp, kp, vp, gp, *, seq):
    @pl.when(pl.program_id(0) == 0)
    def _():
        qp[...] = jnp.zeros_like(qp)
        kp[...] = jnp.zeros_like(kp)
        vp[...] = jnp.zeros_like(vp)
        gp[...] = jnp.zeros_like(gp)

    for s in range(MS_NB):
        rows = slice(s * seq, (s + 1) * seq)
        qp[0:seq, :] = q_ref[rows, :]
        kp[0:seq, :] = k_ref[rows, :]
        vp[0:seq, :] = v_ref[rows, :]
        gp[0:seq, :] = g_ref[rows, :]
        li, b, lit, bt = _mlstm_gates(gp[...], bias_ref[...], 0, seq)
        for h in range(M_H):
            q = qp[:, h * M_DK:(h + 1) * M_DK].astype(BF16)
            k = kp[:, h * M_DK:(h + 1) * M_DK].astype(BF16)
            v = vp[:, h * M_DV:(h + 1) * M_DV].astype(BF16)
            h_out, c_new, n_new, m_new = _mlstm_head(
                q, k, v, li, b, lit, bt, h, c_in[s, h], n_in[s, h], m_in[s, h][:, 0:1])
            c_out[s, h] = c_new
            n_out[s, h] = n_new
            m_out[s, h] = jnp.broadcast_to(m_new, (1, LANE))
            sl = slice(h * M_DV, (h + 1) * M_DV)
            mixed = _mlstm_mix(h_out[0:seq], mo_ref[rows, sl], mhg_ref[:, sl])
            out_ref[rows, sl] = mixed.astype(out_ref.dtype)


def _mlstm_sample(slab_f, bias_row, mhg, c0, n0, m0, seq):
    nb = c0.shape[0]
    rb = MS_NB * seq
    st4 = lambda i: (i, 0, 0, 0)
    return pl.pallas_call(
        functools.partial(_mlstm_sample_kernel, seq=seq),
        grid=(nb // MS_NB,),
        in_specs=[pl.BlockSpec((rb, M_H * M_DK), lambda i: (i, COL_MQ // 512)),
                  pl.BlockSpec((rb, M_H * M_DK), lambda i: (i, COL_MK // 512)),
                  pl.BlockSpec((rb, M_H * M_DV), lambda i: (i, COL_MV // 1024)),
                  pl.BlockSpec((rb, M_H * M_DV), lambda i: (i, COL_MO // 1024)),
                  pl.BlockSpec((rb, LANE), lambda i: (i, COL_SM // LANE)),
                  pl.BlockSpec((1, LANE), lambda i: (0, 0)),
                  pl.BlockSpec((1, M_H * M_DV), lambda i: (0, 0)),
                  pl.BlockSpec((MS_NB, M_H, M_DK, M_DV), st4),
                  pl.BlockSpec((MS_NB, M_H, 1, M_DK), st4),
                  pl.BlockSpec((MS_NB, M_H, 1, LANE), st4)],
        out_specs=[pl.BlockSpec((rb, M_H * M_DV), lambda i: (i, 0)),
                   pl.BlockSpec((MS_NB, M_H, M_DK, M_DV), st4),
                   pl.BlockSpec((MS_NB, M_H, 1, M_DK), st4),
                   pl.BlockSpec((MS_NB, M_H, 1, LANE), st4)],
        out_shape=[jax.ShapeDtypeStruct((nb * seq, M_H * M_DV), BF16),
                   jax.ShapeDtypeStruct((nb, M_H, M_DK, M_DV), F32),
                   jax.ShapeDtypeStruct((nb, M_H, 1, M_DK), F32),
                   jax.ShapeDtypeStruct((nb, M_H, 1, LANE), F32)],
        scratch_shapes=[pltpu.VMEM((LANE, M_H * M_DK), F32),
                        pltpu.VMEM((LANE, M_H * M_DK), F32),
                        pltpu.VMEM((LANE, M_H * M_DV), F32),
                        pltpu.VMEM((LANE, LANE), F32)],
        compiler_params=_cparams(("arbitrary",)),
        name="mlstm_sample",
    )(slab_f, slab_f, slab_f, slab_f, slab_f, bias_row, mhg, c0, n0, m0)


def _sortable(x):
    x = jnp.where(x == 0.0, 0.0, x)
    bits = lax.bitcast_convert_type(x, I32)
    return bits ^ ((bits >> 31) & jnp.int32(0x7FFFFFFF))


def _count(keys_ref, ntiles, pred):
    shape = keys_ref.shape[1:]

    def body(j, acc):
        return acc + jnp.where(pred(j, keys_ref[j]), 1.0, 0.0)

    acc = lax.fori_loop(0, ntiles, body, jnp.zeros(shape, F32))
    return jnp.broadcast_to(jnp.sum(acc, axis=-1, keepdims=True), shape)


def _topk_select(keys_ref, ntiles, topk):
    shape = keys_ref.shape[1:]
    kf = float(topk)

    def bit_body(it, thr):
        cand = thr + lax.shift_left(jnp.int32(1), 31 - it)
        cnt = _count(keys_ref, ntiles, lambda j, x: x >= cand)
        return jnp.where(cnt >= kf, cand, thr)

    thr = lax.fori_loop(0, 32, bit_body, jnp.full(shape, INT_MIN, I32))
    n_gt = _count(keys_ref, ntiles, lambda j, x: x > thr)
    n_ge = _count(keys_ref, ntiles, lambda j, x: x >= thr)
    need = kf - n_gt
    lane = lax.broadcasted_iota(I32, shape, len(shape) - 1)
    nbits = max(1, int(keys_ref.shape[0] * LANE - 1).bit_length())

    def tie_fn():
        def pos_body(it, qp):
            cand = qp + lax.shift_left(jnp.int32(1), nbits - 1 - it)
            cnt = _count(keys_ref, ntiles, lambda j, x: (x == thr) & (j * LANE + lane < cand))
            return jnp.where(cnt <= need - 1.0, cand, qp)

        return lax.fori_loop(0, nbits, pos_body, jnp.zeros(shape, I32))

    def no_tie_fn():
        return jnp.full(shape, 2 ** nbits - 1, I32)

    has_tie = jnp.max(n_ge) > kf
    qpos = lax.cond(has_tie, tie_fn, no_tie_fn)
    return thr, qpos


def _selected(keys, j, thr, qpos):
    lane = lax.broadcasted_iota(I32, keys.shape, len(keys.shape) - 1)
    tie_ok = (keys == thr) & (j * LANE + lane <= qpos) & (keys > INT_MIN)
    return (keys > thr) | tie_ok


def _flash_step(h, s, sel, v, m_scr, l_scr, acc_scr):
    s = jnp.where(sel, s, NEG_BIG)
    m_old = m_scr[h]
    m_new = jnp.maximum(m_old, jnp.max(s, axis=-1, keepdims=True))
    p = jnp.where(sel, jnp.exp(s - m_new), 0.0)
    alpha = jnp.exp(m_old - m_new)
    l_scr[h] = alpha * l_scr[h] + jnp.sum(p, axis=-1, keepdims=True)
    acc_scr[h] = alpha * acc_scr[h] + _dot(p.astype(BF16), v)
    m_scr[h] = m_new


def _dsa_prompt_kernel(q_ref, iq_ref, g_ref, k_ref, v_ref, ki_ref, bias_ref, o_ref,
                       keys_scr, m_scr, l_scr, acc_scr, *, topk):
    i = pl.program_id(1)
    ntiles = i + 2
    row = lax.broadcasted_iota(I32, (LANE, LANE), 0)
    lane = lax.broadcasted_iota(I32, (LANE, LANE), 1)
    qpos = N_META + LANE * i + row
    w_all = g_ref[:, SM_IW:SM_IW + IDX_H] * (IDX_D ** -0.5 * IDX_H ** -0.5)

    def phys_row(j):
        return pl.multiple_of(jnp.where(j == 0, META_TILE, j - 1) * LANE, LANE)

    def score_body(j, carry):
        kt = ki_ref[pl.ds(phys_row(j), LANE), SM_IK:SM_IK + IDX_D]
        sc = jnp.zeros((LANE, LANE), F32)
        for h in range(IDX_H):
            d = _dot_nt(iq_ref[:, h * IDX_D:(h + 1) * IDX_D], kt)
            sc = sc + jnp.maximum(d, 0.0) * w_all[:, h:h + 1]
        kpos = N_META + LANE * (j - 1) + lane
        valid = (kpos >= 0) & (kpos <= qpos)
        keys_scr[j] = jnp.where(valid, _sortable(sc), INT_MIN)
        return carry

    lax.fori_loop(0, ntiles, score_body, 0)
    thr, tie_pos = _topk_select(keys_scr, ntiles, topk)

    m_scr[...] = jnp.full(m_scr.shape, NEG_BIG, F32)
    l_scr[...] = jnp.zeros_like(l_scr)
    acc_scr[...] = jnp.zeros_like(acc_scr)

    def att_body(j, carry):
        r0 = phys_row(j)
        sel = _selected(keys_scr[j], j, thr, tie_pos)
        dsel = jnp.minimum(i - (j - 1), 2)
        for h in range(A_H):
            sl = slice(h * A_DH, (h + 1) * A_DH)
            s = _dot_nt(q_ref[:, sl], k_ref[pl.ds(r0, LANE), sl]) * (A_DH ** -0.5) + bias_ref[dsel, h]
            _flash_step(h, s, sel, v_ref[pl.ds(r0, LANE), sl], m_scr, l_scr, acc_scr)
        return carry

    lax.fori_loop(0, ntiles, att_body, 0)
    for h in range(A_H):
        o_ref[:, h * A_DH:(h + 1) * A_DH] = (acc_scr[h] / l_scr[h]).astype(o_ref.dtype)


def _dsa_prompt(slab_b, slab_f, bias_tiles, topk):
    nb = slab_b.shape[0]
    nq = (T_PAD - LANE) // LANE
    wide = A_H * A_DH
    return pl.pallas_call(
        functools.partial(_dsa_prompt_kernel, topk=topk),
        grid=(nb, nq),
        in_specs=[pl.BlockSpec((None, LANE, wide), lambda b, i: (b, i, COL_AQ // wide)),
                  pl.BlockSpec((None, LANE, wide), lambda b, i: (b, i, COL_IQ // wide)),
                  pl.BlockSpec((None, LANE, LANE), lambda b, i: (b, i, COL_SM // LANE)),
                  pl.BlockSpec((None, T_PAD, wide), lambda b, i: (b, 0, COL_AK // wide)),
                  pl.BlockSpec((None, T_PAD, wide), lambda b, i: (b, 0, COL_AV // wide)),
                  pl.BlockSpec((None, T_PAD, LANE), lambda b, i: (b, 0, COL_SM // LANE)),
                  _resident((3, A_H, LANE, LANE), lambda b, i: (0, 0, 0, 0))],
        out_specs=pl.BlockSpec((None, LANE, wide), lambda b, i: (b, i, 0)),
        out_shape=jax.ShapeDtypeStruct((nb, T_PAD - LANE, wide), BF16),
        scratch_shapes=[pltpu.VMEM((nq + 1, LANE, LANE), I32),
                        pltpu.VMEM((A_H, LANE, LANE), F32),
                        pltpu.VMEM((A_H, LANE, LANE), F32),
                        pltpu.VMEM((A_H, LANE, A_DH), F32)],
        compiler_params=_cparams(("arbitrary", "arbitrary")),
        name="dsa_prompt",
    )(slab_b, slab_b, slab_f, slab_b, slab_b, slab_b, bias_tiles)


def _dsa_sample_score_kernel(pt_ref, iq_ref, g_ref, kp_ref, o_ref, kpad, *, seq, n_pages):
    b, p = pl.program_id(0), pl.program_id(1)

    @pl.when((b == 0) & (p == 0))
    def _():
        kpad[...] = jnp.zeros_like(kpad)

    row = lax.broadcasted_iota(I32, (seq, LANE), 0)
    lane = lax.broadcasted_iota(I32, (seq, LANE), 1)
    w_all = g_ref[:, SM_IW:SM_IW + IDX_H] * (IDX_D ** -0.5 * IDX_H ** -0.5)

    def scores(kt):
        sc = jnp.zeros((seq, LANE), F32)
        for h in range(IDX_H):
            d = _dot_nt(iq_ref[:, h * IDX_D:(h + 1) * IDX_D].astype(BF16), kt)
            sc = sc + jnp.maximum(d, 0.0) * w_all[:, h:h + 1]
        return sc

    @pl.when(p < n_pages)
    def _():
        o_ref[...] = _sortable(scores(kp_ref[...].astype(BF16)))

    @pl.when(p == n_pages)
    def _():
        kpad[0:seq, :] = g_ref[:, SM_IK:SM_IK + IDX_D]
        sc = scores(kpad[...].astype(BF16))
        o_ref[...] = jnp.where(lane <= row, _sortable(sc), INT_MIN)


def _dsa_sample_scores(page_table, slab_f, kidx_pool, seq):
    nb, n_pages = page_table.shape
    grid_spec = pltpu.PrefetchScalarGridSpec(
        num_scalar_prefetch=1,
        grid=(nb, n_pages + 1),
        in_specs=[pl.BlockSpec((seq, IDX_H * IDX_D), lambda b, p, pt: (b, COL_IQ // 1024)),
                  pl.BlockSpec((seq, LANE), lambda b, p, pt: (b, COL_SM // LANE)),
                  pl.BlockSpec((None, PAGE, IDX_D),
                               lambda b, p, pt: (pt[b, jnp.minimum(p, n_pages - 1)], 0, 0))],
        out_specs=pl.BlockSpec((None, seq, LANE), lambda b, p, pt: (p, b, 0)),
        scratch_shapes=[pltpu.VMEM((LANE, IDX_D), F32)],
    )
    return pl.pallas_call(
        functools.partial(_dsa_sample_score_kernel, seq=seq, n_pages=n_pages),
        grid_spec=grid_spec,
        out_shape=jax.ShapeDtypeStruct((n_pages + 1, nb * seq, LANE), I32),
        compiler_params=_cparams(("arbitrary", "arbitrary")),
        name="dsa_sample_scores",
    )(page_table, slab_f, slab_f, kidx_pool)


def _dsa_select_kernel(keys_ref, thr_ref, pos_ref, *, topk):
    thr, tie_pos = _topk_select(keys_ref, keys_ref.shape[0], topk)
    thr_ref[...] = thr
    pos_ref[...] = tie_pos


def _dsa_select(keys, topk):
    nt, rows, _ = keys.shape
    return pl.pallas_call(
        functools.partial(_dsa_select_kernel, topk=topk),
        grid=(rows // LANE,),
        in_specs=[pl.BlockSpec((nt, LANE, LANE), lambda r: (0, r, 0))],
        out_specs=[pl.BlockSpec((LANE, LANE), lambda r: (r, 0)),
                   pl.BlockSpec((LANE, LANE), lambda r: (r, 0))],
        out_shape=[jax.ShapeDtypeStruct((rows, LANE), I32),
                   jax.ShapeDtypeStruct((rows, LANE), I32)],
        compiler_params=_cparams(("arbitrary",)),
        name="dsa_select",
    )(keys)


def _dsa_sample_attend_kernel(pt_ref, q_ref, nk_ref, nv_ref, kp_ref, vp_ref, keys_ref, thr_ref, pos_ref,
                              bias_ref, o_ref, kpad, vpad, m_scr, l_scr, acc_scr, *, seq, n_pages):
    b, p = pl.program_id(0), pl.program_id(1)

    @pl.when((b == 0) & (p == 0))
    def _():
        kpad[...] = jnp.zeros_like(kpad)
        vpad[...] = jnp.zeros_like(vpad)

    @pl.when(p == 0)
    def _():
        m_scr[...] = jnp.full(m_scr.shape, NEG_BIG, F32)
        l_scr[...] = jnp.zeros_like(l_scr)
        acc_scr[...] = jnp.zeros_like(acc_scr)

    sel = _selected(keys_ref[...], p, thr_ref[...], pos_ref[...])
    dsel = jnp.minimum(n_pages - p, 2)

    def attend(k_tile, v_tile):
        for h in range(A_H):
            sl = slice(h * A_DH, (h + 1) * A_DH)
            s = (_dot_nt(q_ref[:, sl].astype(BF16), k_tile[:, sl].astype(BF16)) * (A_DH ** -0.5)
                 + bias_ref[dsel, h, 0:seq, :])
            _flash_step(h, s, sel, v_tile[:, sl].astype(BF16), m_scr, l_scr, acc_scr)

    @pl.when(p < n_pages)
    def _():
        attend(kp_ref, vp_ref)

    @pl.when(p == n_pages)
    def _():
        kpad[0:seq, :] = nk_ref[...]
        vpad[0:seq, :] = nv_ref[...]
        attend(kpad, vpad)
        for h in range(A_H):
            o_ref[:, h * A_DH:(h + 1) * A_DH] = (acc_scr[h] / l_scr[h]).astype(o_ref.dtype)


def _dsa_sample_attend(page_table, slab_f, k_pool, v_pool, keys, thr, tie_pos, bias_tiles, seq):
    nb, n_pages = page_table.shape
    wide = A_H * A_DH
    page_map = lambda b, p, pt: (pt[b, jnp.minimum(p, n_pages - 1)], 0, 0)
    grid_spec = pltpu.PrefetchScalarGridSpec(
        num_scalar_prefetch=1,
        grid=(nb, n_pages + 1),
        in_specs=[pl.BlockSpec((seq, wide), lambda b, p, pt: (b, COL_AQ // wide)),
                  pl.BlockSpec((seq, wide), lambda b, p, pt: (b, COL_AK // wide)),
                  pl.BlockSpec((seq, wide), lambda b, p, pt: (b, COL_AV // wide)),
                  pl.BlockSpec((None, PAGE, wide), page_map),
                  pl.BlockSpec((None, PAGE, wide), page_map),
                  pl.BlockSpec((None, seq, LANE), lambda b, p, pt: (p, b, 0)),
                  pl.BlockSpec((seq, LANE), lambda b, p, pt: (b, 0)),
                  pl.BlockSpec((seq, LANE), lambda b, p, pt: (b, 0)),
                  _resident((3, A_H, LANE, LANE), lambda b, p, pt: (0, 0, 0, 0))],
        out_specs=pl.BlockSpec((seq, wide), lambda b, p, pt: (b, 0)),
        scratch_shapes=[pltpu.VMEM((LANE, wide), F32),
                        pltpu.VMEM((LANE, wide), F32),
                        pltpu.VMEM((A_H, seq, LANE), F32),
                        pltpu.VMEM((A_H, seq, LANE), F32),
                        pltpu.VMEM((A_H, seq, A_DH), F32)],
    )
    return pl.pallas_call(
        functools.partial(_dsa_sample_attend_kernel, seq=seq, n_pages=n_pages),
        grid_spec=grid_spec,
        out_shape=jax.ShapeDtypeStruct((nb * seq, wide), F32),
        compiler_params=_cparams(("arbitrary", "arbitrary")),
        name="dsa_sample_attend",
    )(page_table, slab_f, slab_f, slab_f, k_pool, v_pool, keys, thr, tie_pos, bias_tiles)


MO_TM = 256


def _mixout_kernel(mm_ref, ma_ref, x_ref, wo_ref, g_ref, wq_ref, k1_ref, k2_ref,
                   hp_ref, xnt_ref, s1_ref, s2_ref):
    half = M_H * M_DV
    y = (_dot(mm_ref[...].astype(BF16), wo_ref[0:half, :])
         + _dot(ma_ref[...].astype(BF16), wo_ref[half:, :]))
    hp = x_ref[...] + y
    hp_ref[...] = hp
    xn = hp * lax.rsqrt(jnp.mean(hp * hp, axis=-1, keepdims=True) + RMS_EPS) * g_ref[...]
    xnt = xn.T.astype(BF16)
    xnt_ref[...] = xnt
    qqt = _dot(wq_ref[...], xnt)
    hq = PEER_DQ // 2
    for h in range(PEER_H):
        q1 = qqt[h * PEER_DQ:h * PEER_DQ + hq, :].astype(BF16)
        q2 = qqt[h * PEER_DQ + hq:(h + 1) * PEER_DQ, :].astype(BF16)
        s1_ref[h] = _dot(k1_ref[h], q1)
        s2_ref[h] = _dot(k2_ref[h], q2)


def _mixout(mm, ma, x2d, wo, g2, wqt, k1, k2):
    rows = x2d.shape[0]
    half = M_H * M_DV
    c2 = lambda i: (0, 0)
    c3 = lambda i: (0, 0, 0)
    return pl.pallas_call(
        _mixout_kernel,
        grid=(rows // MO_TM,),
        in_specs=[pl.BlockSpec((MO_TM, half), lambda i: (i, 0)),
                  pl.BlockSpec((MO_TM, A_H * A_DH), lambda i: (i, 0)),
                  pl.BlockSpec((MO_TM, D_MODEL), lambda i: (i, 0)),
                  _resident((half + A_H * A_DH, D_MODEL), c2),
                  _resident((1, D_MODEL), c2),
                  _resident((PEER_H * PEER_DQ, D_MODEL), c2),
                  _resident((PEER_H, N_KEYS, PEER_DQ // 2), c3),
                  _resident((PEER_H, N_KEYS, PEER_DQ // 2), c3)],
        out_specs=[pl.BlockSpec((MO_TM, D_MODEL), lambda i: (i, 0)),
                   pl.BlockSpec((D_MODEL, MO_TM), lambda i: (0, i)),
                   pl.BlockSpec((PEER_H, N_KEYS, MO_TM), lambda i: (0, 0, i)),
                   pl.BlockSpec((PEER_H, N_KEYS, MO_TM), lambda i: (0, 0, i))],
        out_shape=[jax.ShapeDtypeStruct((rows, D_MODEL), F32),
                   jax.ShapeDtypeStruct((D_MODEL, rows), BF16),
                   jax.ShapeDtypeStruct((PEER_H, N_KEYS, rows), F32),
                   jax.ShapeDtypeStruct((PEER_H, N_KEYS, rows), F32)],
        compiler_params=_cparams(("arbitrary",)),
        name="mixout",
    )(mm, ma, x2d, wo, g2, wqt, k1, k2)


PS_TN = 256


def _extract_top(s, count):
    n = s.shape[0]
    idx = lax.broadcasted_iota(I32, s.shape, 0)
    rank = jnp.full(s.shape, float(count), F32)
    rem = s
    vals = []
    for r in range(count):
        mx = jnp.max(rem, axis=0, keepdims=True)
        first = jnp.min(jnp.where(rem == mx, idx, n), axis=0, keepdims=True)
        hit = idx == first
        rank = jnp.where(hit, float(r), rank)
        rem = jnp.where(hit, -jnp.inf, rem)
        vals.append(mx)
    return vals, rank


def _peer_select_kernel(s1_ref, s2_ref, li_ref, r2_ref, c1_ref, e2_ref):
    s1 = s1_ref[...]
    s2 = s2_ref[...]
    v1, rank1 = _extract_top(s1, PEER_TOPK)
    v2, rank2 = _extract_top(s2, PEER_TOPK)
    r16 = lax.broadcasted_iota(I32, (PEER_TOPK, s2.shape[1]), 0)
    v2_all = jnp.zeros((PEER_TOPK, s2.shape[1]), F32)
    for r in range(PEER_TOPK):
        v2_all = jnp.where(r16 == r, v2[r], v2_all)
    cand = jnp.concatenate([v1[r] + v2_all for r in range(PEER_TOPK)], axis=0)
    sc, crank = _extract_top(cand, PEER_TOPK)
    chosen = jnp.where(crank < float(PEER_TOPK), 1.0, 0.0)
    li = jnp.zeros(s1.shape, F32)
    for r in range(PEER_TOPK):
        cnt = jnp.sum(chosen[r * PEER_TOPK:(r + 1) * PEER_TOPK], axis=0, keepdims=True)
        li = li + jnp.where(rank1 == float(r), cnt, 0.0)
    z = jnp.zeros_like(sc[0])
    for c in range(PEER_TOPK):
        z = z + jnp.exp(sc[c] - sc[0])
    li_ref[...] = li
    r2_ref[...] = rank2
    c1_ref[...] = jnp.exp(s1 - v1[0]) / z
    e2_ref[...] = jnp.exp(s2 - v2[0])


def _peer_select(s1t, s2t):
    nh, nk, rows = s1t.shape
    spec = pl.BlockSpec((None, nk, PS_TN), lambda t, h: (h, 0, t))
    shp = jax.ShapeDtypeStruct((nh, nk, rows), F32)
    return pl.pallas_call(
        _peer_select_kernel,
        grid=(rows // PS_TN, nh),
        in_specs=[spec, spec],
        out_specs=[spec, spec, spec, spec],
        out_shape=[shp, shp, shp, shp],
        compiler_params=_cparams(("arbitrary", "arbitrary")),
        name="peer_select",
    )(s1t, s2t)


PD_TN = 512
PD_NI = 8
PD_EB = PD_NI * N_KEYS


def _peer_dense_kernel(xnt_ref, u_ref, vt_ref, li_ref, r2_ref, c1_ref, e2_ref, o_ref,
                       acc_scr, a_scr, p_scr):
    e = pl.program_id(1)

    @pl.when(e == 0)
    def _():
        acc_scr[...] = jnp.zeros_like(acc_scr)

    a_scr[...] = _dot(u_ref[...], xnt_ref[...])
    for il in range(PD_NI):
        rows = slice(il * N_KEYS, (il + 1) * N_KEYS)
        for tc in range(PD_TN // LANE):
            cols = slice(tc * LANE, (tc + 1) * LANE)
            w = jnp.zeros((N_KEYS, LANE), F32)
            for h in range(PEER_H):
                li = li_ref[h, il:il + 1, cols]
                c1 = c1_ref[h, il:il + 1, cols]
                w = w + jnp.where(r2_ref[h, :, cols] < li, e2_ref[h, :, cols], 0.0) * c1
            a = a_scr[rows, cols]
            act = 0.5 * a * (1.0 + lax.erf(a * (2.0 ** -0.5)))
            p_scr[rows, cols] = (w * act).astype(BF16)
    acc_scr[...] += _dot(vt_ref[...], p_scr[...])

    @pl.when(e == pl.num_programs(1) - 1)
    def _():
        o_ref[...] = acc_scr[...].T


def _peer_dense(xnt, u, vt, li, r2, c1, e2):
    rows = xnt.shape[1]
    key2_spec = pl.BlockSpec((PEER_H, N_KEYS, PD_TN), lambda t, e: (0, 0, t))
    key1_spec = pl.BlockSpec((PEER_H, PD_NI, PD_TN), lambda t, e: (0, e, t))
    return pl.pallas_call(
        _peer_dense_kernel,
        grid=(rows // PD_TN, N_EXPERTS // PD_EB),
        in_specs=[pl.BlockSpec((D_MODEL, PD_TN), lambda t, e: (0, t)),
                  pl.BlockSpec((PD_EB, D_MODEL), lambda t, e: (e, 0)),
                  pl.BlockSpec((D_MODEL, PD_EB), lambda t, e: (0, e)),
                  key1_spec, key2_spec, key1_spec, key2_spec],
        out_specs=pl.BlockSpec((PD_TN, D_MODEL), lambda t, e: (t, 0)),
        out_shape=jax.ShapeDtypeStruct((rows, D_MODEL), F32),
        scratch_shapes=[pltpu.VMEM((D_MODEL, PD_TN), F32),
                        pltpu.VMEM((PD_EB, PD_TN), F32),
                        pltpu.VMEM((PD_EB, PD_TN), BF16)],
        compiler_params=_cparams(("arbitrary", "arbitrary")),
        name="peer_dense",
    )(xnt, u, vt, li, r2, c1, e2)


def _final_kernel(hp_ref, po_ref, g_ref, y_ref):
    h = hp_ref[...] + po_ref[...]
    y_ref[...] = h * lax.rsqrt(jnp.mean(h * h, axis=-1, keepdims=True) + RMS_EPS) * g_ref[...]


def _final(hp, po, g):
    rows = hp.shape[0]
    spec = pl.BlockSpec((MO_TM, D_MODEL), lambda i: (i, 0))
    return pl.pallas_call(
        _final_kernel,
        grid=(rows // MO_TM,),
        in_specs=[spec, spec, pl.BlockSpec((1, D_MODEL), lambda i: (0, 0))],
        out_specs=spec,
        out_shape=jax.ShapeDtypeStruct((rows, D_MODEL), F32),
        compiler_params=_cparams(("arbitrary",)),
        name="final_norm",
    )(hp, po, g)


def _t5_bucket_table(n):
    rel = np.arange(n)
    max_exact = REL_BUCKETS // 2
    nf = np.maximum(rel, 1).astype(np.float32)
    large = max_exact + (np.log(nf / np.float32(max_exact)) / np.float32(math.log(REL_MAX_DIST / max_exact))
                         * np.float32(REL_BUCKETS - max_exact)).astype(np.int32)
    large = np.minimum(large, REL_BUCKETS - 1)
    return np.where(rel < max_exact, rel, large).astype(np.int32)


def _bias_tiles(rel_bias):
    table = _t5_bucket_table(3 * LANE)
    r = np.arange(LANE)[:, None]
    c = np.arange(LANE)[None, :]
    rel = np.stack([np.maximum(r - c, 0), LANE + r - c, np.full((LANE, LANE), 2 * LANE)])
    assert table[LANE + 1] == REL_BUCKETS - 1
    buckets = table[rel]
    return jnp.transpose(rel_bias.astype(F32)[buckets], (0, 3, 1, 2))


def _relayout_w_in(w):
    cuts = np.cumsum(IN_SPLITS)[:-1].tolist()
    mq, mk, mv, mo, ig, fg, aq, ak, av, iq, ik, iw = jnp.split(w, cuts, axis=-1)
    pad = jnp.zeros((w.shape[0], LANE - IDX_D - IDX_H - 2 * M_H), w.dtype)
    return jnp.concatenate([aq, ak, av, iq, mv, mo, mq, mk, ik, iw, ig, fg, pad], axis=-1).astype(BF16)


def kernel(x_prompt, x_sample, cache_k, cache_v, cache_kidx, state_C, state_n, state_m, page_table,
           meta_tokens, rel_bias, final_norm_g, norm1_g, w_in, b_i, b_f, mh_norm_g, w_out, norm2_g,
           peer_wq, peer_k1, peer_k2, peer_u, peer_v):
    B, S_p, D = x_prompt.shape
    Bs, Ts, _ = x_sample.shape
    n_pages = page_table.shape[1]
    depth = w_in.shape[0]
    assert depth == 1 and D == D_MODEL and N_META + S_p == T_PAD - META_OFF and S_p % LANE == 0
    assert n_pages * PAGE + Ts <= (n_pages + 1) * PAGE and cache_k.shape[2] == PAGE
    topk_p = min(TOPK_MAX, S_p // 4)
    topk_s = min(TOPK_MAX, (n_pages * PAGE + Ts) // 4)
    l = 0

    w1 = _relayout_w_in(w_in[l])
    g1 = norm1_g[l].astype(F32).reshape(1, D)
    g2 = norm2_g[l].astype(F32).reshape(1, D)
    gf = final_norm_g.astype(F32).reshape(1, D)
    zpad = jnp.zeros((LANE - SM_FG - M_H,), F32)
    bias_row = jnp.concatenate([jnp.zeros((SM_IG,), F32), b_i[l].astype(F32), b_f[l].astype(F32), zpad]).reshape(1, LANE)
    mhg = mh_norm_g[l].astype(F32).reshape(1, M_H * M_DV)
    wo = w_out[l].astype(BF16)
    wqt = peer_wq[l].T.astype(BF16)
    k1 = peer_k1[l].astype(BF16)
    k2 = peer_k2[l].astype(BF16)
    u = peer_u[l].astype(BF16)
    vt = peer_v[l].T.astype(BF16)
    bias_tiles = _bias_tiles(rel_bias)

    meta = jnp.broadcast_to(meta_tokens.astype(F32)[None], (B, N_META, D))
    xp_all = jnp.concatenate([x_prompt, jnp.zeros((B, META_OFF, D), F32), meta], axis=1)

    pf, pb = _inproj(xp_all.reshape(B * T_PAD, D), g1, w1)
    pf = pf.reshape(B, T_PAD, P_PAD)
    pb = pb.reshape(B, T_PAD, P_PAD)
    mm_p, c_p, n_p, m_p = _mlstm_prompt(pb, pf, bias_row, mhg)
    ma_p = _dsa_prompt(pb, pf, bias_tiles, topk_p)
    xp2 = x_prompt.reshape(B * S_p, D)
    hp_p, xnt_p, s1_p, s2_p = _mixout(mm_p.reshape(B * S_p, -1), ma_p.reshape(B * S_p, -1), xp2, wo, g2, wqt, k1, k2)
    po_p = _peer_dense(xnt_p, u, vt, *_peer_select(s1_p, s2_p))
    y_prompt = _final(hp_p, po_p, gf).reshape(B, S_p, D)

    xs2 = x_sample.reshape(Bs * Ts, D)
    sf, _ = _inproj(xs2, g1, w1)
    c0 = state_C[l].astype(F32)
    n0 = state_n[l].astype(F32).reshape(Bs, M_H, 1, M_DK)
    m0 = jnp.broadcast_to(state_m[l].astype(F32)[:, :, None, None], (Bs, M_H, 1, LANE))
    mm_s, c_s, n_s, m_s = _mlstm_sample(sf, bias_row, mhg, c0, n0, m0, Ts)
    n_pool = cache_k.shape[1]
    keys_s = _dsa_sample_scores(page_table, sf, cache_kidx[l].reshape(n_pool, PAGE, IDX_D), Ts)
    thr_s, pos_s = _dsa_select(keys_s, topk_s)
    ma_s = _dsa_sample_attend(page_table, sf, cache_k[l].reshape(n_pool, PAGE, A_H * A_DH),
                              cache_v[l].reshape(n_pool, PAGE, A_H * A_DH), keys_s, thr_s, pos_s, bias_tiles, Ts)
    hp_s, xnt_s, s1_s, s2_s = _mixout(mm_s, ma_s, xs2, wo, g2, wqt, k1, k2)
    po_s = _peer_dense(xnt_s, u, vt, *_peer_select(s1_s, s2_s))
    y_sample = _final(hp_s, po_s, gf).reshape(Bs, Ts, D)

    def seq_rows(a):
        return jnp.concatenate([a[:, T_PAD - N_META:], a[:, :S_p]], axis=1)

    wide = A_H * A_DH
    T = N_META + S_p
    k_prompt = seq_rows(pf[:, :, COL_AK:COL_AK + wide]).reshape(1, B, T, A_H, A_DH)
    v_prompt = seq_rows(pf[:, :, COL_AV:COL_AV + wide]).reshape(1, B, T, A_H, A_DH)
    kidx_prompt = seq_rows(pf[:, :, COL_SM + SM_IK:COL_SM + SM_IK + IDX_D]).reshape(1, B, T, IDX_D)
    k_sample = sf[:, COL_AK:COL_AK + wide].reshape(1, Bs, Ts, A_H, A_DH)
    v_sample = sf[:, COL_AV:COL_AV + wide].reshape(1, Bs, Ts, A_H, A_DH)
    kidx_sample = sf[:, COL_SM + SM_IK:COL_SM + SM_IK + IDX_D].reshape(1, Bs, Ts, IDX_D)
    return (y_prompt, y_sample,
            k_prompt, v_prompt, kidx_prompt,
            c_p[None], n_p[:, :, 0, :][None], m_p[:, :, 0, 0][None],
            k_sample, v_sample, kidx_sample,
            c_s[None], n_s[:, :, 0, :][None], m_s[:, :, 0, 0][None])
```

```python
import functools
import math

import numpy as np
import jax
import jax.numpy as jnp
from jax import lax
from jax.experimental import pallas as pl
from jax.experimental.pallas import tpu as pltpu

F32 = jnp.float32
BF16 = jnp.bfloat16
I32 = jnp.int32

D_MODEL = 2048
N_META = 16
RMS_EPS = 1e-6
M_H, M_DK, M_DV = 4, 128, 256
A_H, A_DH = 8, 128
IDX_H, IDX_D = 16, 64
TOPK_MAX = 256
REL_BUCKETS, REL_MAX_DIST = 32, 128
PEER_H, PEER_DQ, N_KEYS, PEER_TOPK = 8, 256, 128, 16
N_EXPERTS = N_KEYS * N_KEYS
PAGE = 128
IN_SPLITS = (M_H * M_DK, M_H * M_DK, M_H * M_DV, M_H * M_DV, M_H, M_H,
             A_H * A_DH, A_H * A_DH, A_H * A_DH, IDX_H * IDX_D, IDX_D, IDX_H)

LANE = 128
VMEM_LIMIT = 56 * 1024 * 1024

COL_AQ, COL_AK, COL_AV, COL_IQ, COL_MV, COL_MO, COL_MQ, COL_MK, COL_SM = (
    0, 1024, 2048, 3072, 4096, 5120, 6144, 6656, 7168)
P_PAD = COL_SM + 512
SM_IK, SM_IW, SM_IG, SM_FG = 0, 64, 80, 84
T_PAD = 17 * LANE
META_TILE = 16
META_OFF = LANE - N_META
INT_MIN = -(2 ** 31)
NEG_BIG = -1e30


def _cparams(sem, vmem=VMEM_LIMIT):
    return pltpu.CompilerParams(dimension_semantics=sem, vmem_limit_bytes=vmem)


def _resident(shape, index_map):
    return pl.BlockSpec(shape, index_map, pipeline_mode=pl.Buffered(1))


def _dot(a, b):
    return jnp.dot(a, b, preferred_element_type=F32)


def _dot_nt(a, b):
    return lax.dot_general(a, b, (((1,), (1,)), ((), ())), preferred_element_type=F32)


def _dot_tn(a, b):
    return lax.dot_general(a, b, (((0,), (0,)), ((), ())), preferred_element_type=F32)


IP_TM_MAX, IP_TN = 1088, 512


def _inproj_kernel(x_ref, g_ref, w_ref, of_ref, ob_ref, hn_ref):
    @pl.when(pl.program_id(1) == 0)
    def _():
        x = x_ref[...]
        y = x * lax.rsqrt(jnp.mean(x * x, axis=-1, keepdims=True) + RMS_EPS) * g_ref[...]
        hn_ref[...] = y.astype(BF16)

    acc = _dot(hn_ref[...], w_ref[...])
    of_ref[...] = acc
    ob_ref[...] = acc.astype(BF16)


def _inproj(x2d, g, w):
    rows = x2d.shape[0]
    tm = IP_TM_MAX if rows % IP_TM_MAX == 0 else min(rows, 1024)
    assert rows % tm == 0
    grid = (rows // tm, P_PAD // IP_TN)
    return pl.pallas_call(
        _inproj_kernel,
        grid=grid,
        in_specs=[pl.BlockSpec((tm, D_MODEL), lambda i, j: (i, 0)),
                  pl.BlockSpec((1, D_MODEL), lambda i, j: (0, 0)),
                  pl.BlockSpec((D_MODEL, IP_TN), lambda i, j: (0, j))],
        out_specs=[pl.BlockSpec((tm, IP_TN), lambda i, j: (i, j)),
                   pl.BlockSpec((tm, IP_TN), lambda i, j: (i, j))],
        out_shape=[jax.ShapeDtypeStruct((rows, P_PAD), F32),
                   jax.ShapeDtypeStruct((rows, P_PAD), BF16)],
        scratch_shapes=[pltpu.VMEM((tm, D_MODEL), BF16)],
        compiler_params=_cparams(("arbitrary", "arbitrary")),
        name="inproj",
    )(x2d, g, w)


def _mlstm_gates(gates, bias_row, lo, hi):
    row = lax.broadcasted_iota(I32, (LANE, LANE), 0)
    col = lax.broadcasted_iota(I32, (LANE, LANE), 1)
    valid = (row >= lo) & (row < hi)
    pre = gates + bias_row
    li = jnp.where(valid, pre, -jnp.inf)
    lf = jnp.where(valid, jax.nn.log_sigmoid(pre), 0.0)
    tril = jnp.where(col <= row, 1.0, 0.0).astype(F32)
    b = jnp.dot(tril, lf, preferred_element_type=F32, precision=lax.Precision.HIGHEST)
    return li, b, li.T, b.T


def _mlstm_head(q, k, v, li, b, lit, bt, h, c_old, n_old, m_old):
    li_col = li[:, SM_IG + h:SM_IG + h + 1]
    b_col = b[:, SM_FG + h:SM_FG + h + 1]
    li_row = lit[SM_IG + h:SM_IG + h + 1, :]
    b_row = bt[SM_FG + h:SM_FG + h + 1, :]
    row = lax.broadcasted_iota(I32, (LANE, LANE), 0)
    col = lax.broadcasted_iota(I32, (LANE, LANE), 1)
    dmat = jnp.where(col <= row, b_col - b_row + li_row, -jnp.inf)
    inter = b_col + m_old
    m_t = jnp.maximum(inter, jnp.max(dmat, axis=-1, keepdims=True))
    w_inter = jnp.exp(inter - m_t)
    scale = M_DK ** -0.5
    s = _dot_nt(q, k) * scale * jnp.exp(dmat - m_t)
    num = w_inter * _dot(q, c_old.astype(BF16)) + _dot(s.astype(BF16), v)
    qf = q.astype(F32)
    den = w_inter * jnp.sum(qf * n_old, axis=-1, keepdims=True) + jnp.sum(s, axis=-1, keepdims=True)
    h_out = num / jnp.maximum(jnp.abs(den), jnp.exp(-m_t))
    b_last = b_col[LANE - 1:LANE, :]
    g_col = b_last - b_col + li_col
    m_new = jnp.maximum(b_last + m_old, jnp.max(g_col, axis=0, keepdims=True))
    a = jnp.exp(b_last + m_old - m_new)
    wk = jnp.exp(g_col - m_new)
    wv = (wk * v.astype(F32)).astype(BF16)
    c_new = a * c_old + scale * _dot_tn(k, wv)
    n_new = a * n_old + scale * jnp.sum(wk * k.astype(F32), axis=0, keepdims=True)
    return h_out, c_new, n_new, m_new


def _mlstm_mix(h_out, mo, g):
    hn = h_out * lax.rsqrt(jnp.mean(h_out * h_out, axis=-1, keepdims=True) + RMS_EPS)
    return hn * g * jax.nn.sigmoid(mo)


def _mlstm_prompt_kernel(q_ref, k_ref, v_ref, mo_ref, g_ref, bias_ref, mhg_ref,
                         out_ref, c_out, n_out, m_out, c_scr, n_scr, m_scr):
    c = pl.program_id(1)

    @pl.when(c == 0)
    def _():
        c_scr[...] = jnp.zeros_like(c_scr)
        n_scr[...] = jnp.zeros_like(n_scr)
        m_scr[...] = jnp.zeros_like(m_scr)

    lo = jnp.where(c == 0, META_OFF, 0)
    li, b, lit, bt = _mlstm_gates(g_ref[...], bias_ref[...], lo, LANE)
    for h in range(M_H):
        q = q_ref[:, h * M_DK:(h + 1) * M_DK]
        k = k_ref[:, h * M_DK:(h + 1) * M_DK]
        v = v_ref[:, h * M_DV:(h + 1) * M_DV]
        h_out, c_new, n_new, m_new = _mlstm_head(
            q, k, v, li, b, lit, bt, h, c_scr[h], n_scr[h], m_scr[h][:, 0:1])
        c_scr[h] = c_new
        n_scr[h] = n_new
        m_scr[h] = jnp.broadcast_to(m_new, (1, LANE))
        sl = slice(h * M_DV, (h + 1) * M_DV)
        out_ref[:, sl] = _mlstm_mix(h_out, mo_ref[:, sl], mhg_ref[:, sl]).astype(out_ref.dtype)

    @pl.when(c == pl.num_programs(1) - 1)
    def _():
        c_out[...] = c_scr[...]
        n_out[...] = n_scr[...]
        m_out[...] = m_scr[...]


def _mlstm_prompt(slab_b, slab_f, bias_row, mhg):
    nb = slab_b.shape[0]
    nchunk = T_PAD // LANE

    def tile(c):
        return (c + META_TILE) % nchunk

    return pl.pallas_call(
        _mlstm_prompt_kernel,
        grid=(nb, nchunk),
        in_specs=[pl.BlockSpec((None, LANE, M_H * M_DK), lambda b, c: (b, tile(c), COL_MQ // 512)),
                  pl.BlockSpec((None, LANE, M_H * M_DK), lambda b, c: (b, tile(c), COL_MK // 512)),
                  pl.BlockSpec((None, LANE, M_H * M_DV), lambda b, c: (b, tile(c), COL_MV // 1024)),
                  pl.BlockSpec((None, LANE, M_H * M_DV), lambda b, c: (b, tile(c), COL_MO // 1024)),
                  pl.BlockSpec((None, LANE, LANE), lambda b, c: (b, tile(c), COL_SM // LANE)),
                  pl.BlockSpec((1, LANE), lambda b, c: (0, 0)),
                  pl.BlockSpec((1, M_H * M_DV), lambda b, c: (0, 0))],
        out_specs=[pl.BlockSpec((None, LANE, M_H * M_DV), lambda b, c: (b, jnp.maximum(c - 1, 0), 0)),
                   pl.BlockSpec((None, M_H, M_DK, M_DV), lambda b, c: (b, 0, 0, 0)),
                   pl.BlockSpec((None, M_H, 1, M_DK), lambda b, c: (b, 0, 0, 0)),
                   pl.BlockSpec((None, M_H, 1, LANE), lambda b, c: (b, 0, 0, 0))],
        out_shape=[jax.ShapeDtypeStruct((nb, T_PAD - LANE, M_H * M_DV), BF16),
                   jax.ShapeDtypeStruct((nb, M_H, M_DK, M_DV), F32),
                   jax.ShapeDtypeStruct((nb, M_H, 1, M_DK), F32),
                   jax.ShapeDtypeStruct((nb, M_H, 1, LANE), F32)],
        scratch_shapes=[pltpu.VMEM((M_H, M_DK, M_DV), F32),
                        pltpu.VMEM((M_H, 1, M_DK), F32),
                        pltpu.VMEM((M_H, 1, LANE), F32)],
        compiler_params=_cparams(("arbitrary", "arbitrary")),
        name="mlstm_prompt",
    )(slab_b, slab_b, slab_b, slab_f, slab_f, bias_row, mhg)


MS_NB = 2


def _mlstm_sample_kernel(q_ref, k_ref, v_ref, mo_ref, g_ref, bias_ref, mhg_ref, c_in, n_in, m_in,
                         out_ref, c_out, n_out, m_out, qp, kp, vp, gp, *, seq):
    @pl.when(pl.program_id(0) == 0)
    def _():
        qp[...] = jnp.zeros_like(qp)
        kp[...] = jnp.zeros_like(kp)
        vp[...] = jnp.zeros_like(vp)
        gp[...] = jnp.zeros_like(gp)

    for s in range(MS_NB):
        rows = slice(s * seq, (s + 1) * seq)
        qp[0:seq, :] = q_ref[rows, :]
        kp[0:seq, :] = k_ref[rows, :]
        vp[0:seq, :] = v_ref[rows, :]
        gp[0:seq, :] = g_ref[rows, :]
        li, b, lit, bt = _mlstm_gates(gp[...], bias_ref[...], 0, seq)
        for h in range(M_H):
            q = qp[:, h * M_DK:(h + 1) * M_DK].astype(BF16)
            k = kp[:, h * M_DK:(h + 1) * M_DK].astype(BF16)
            v = vp[:, h * M_DV:(h + 1) * M_DV].astype(BF16)
            h_out, c_new, n_new, m_new = _mlstm_head(
                q, k, v, li, b, lit, bt, h, c_in[s, h], n_in[s, h], m_in[s, h][:, 0:1])
            c_out[s, h] = c_new
            n_out[s, h] = n_new
            m_out[s, h] = jnp.broadcast_to(m_new, (1, LANE))
            sl = slice(h * M_DV, (h + 1) * M_DV)
            mixed = _mlstm_mix(h_out[0:seq], mo_ref[rows, sl], mhg_ref[:, sl])
            out_ref[rows, sl] = mixed.astype(out_ref.dtype)


def _mlstm_sample(slab_f, bias_row, mhg, c0, n0, m0, seq):
    nb = c0.shape[0]
    rb = MS_NB * seq
    st4 = lambda i: (i, 0, 0, 0)
    return pl.pallas_call(
        functools.partial(_mlstm_sample_kernel, seq=seq),
        grid=(nb // MS_NB,),
        in_specs=[pl.BlockSpec((rb, M_H * M_DK), lambda i: (i, COL_MQ // 512)),
                  pl.BlockSpec((rb, M_H * M_DK), lambda i: (i, COL_MK // 512)),
                  pl.BlockSpec((rb, M_H * M_DV), lambda i: (i, COL_MV // 1024)),
                  pl.BlockSpec((rb, M_H * M_DV), lambda i: (i, COL_MO // 1024)),
                  pl.BlockSpec((rb, LANE), lambda i: (i, COL_SM // LANE)),
                  pl.BlockSpec((1, LANE), lambda i: (0, 0)),
                  pl.BlockSpec((1, M_H * M_DV), lambda i: (0, 0)),
                  pl.BlockSpec((MS_NB, M_H, M_DK, M_DV), st4),
                  pl.BlockSpec((MS_NB, M_H, 1, M_DK), st4),
                  pl.BlockSpec((MS_NB, M_H, 1, LANE), st4)],
        out_specs=[pl.BlockSpec((rb, M_H * M_DV), lambda i: (i, 0)),
                   pl.BlockSpec((MS_NB, M_H, M_DK, M_DV), st4),
                   pl.BlockSpec((MS_NB, M_H, 1, M_DK), st4),
                   pl.BlockSpec((MS_NB, M_H, 1, LANE), st4)],
        out_shape=[jax.ShapeDtypeStruct((nb * seq, M_H * M_DV), BF16),
                   jax.ShapeDtypeStruct((nb, M_H, M_DK, M_DV), F32),
                   jax.ShapeDtypeStruct((nb, M_H, 1, M_DK), F32),
                   jax.ShapeDtypeStruct((nb, M_H, 1, LANE), F32)],
        scratch_shapes=[pltpu.VMEM((LANE, M_H * M_DK), F32),
                        pltpu.VMEM((LANE, M_H * M_DK), F32),
                        pltpu.VMEM((LANE, M_H * M_DV), F32),
                        pltpu.VMEM((LANE, LANE), F32)],
        compiler_params=_cparams(("arbitrary",)),
        name="mlstm_sample",
    )(slab_f, slab_f, slab_f, slab_f, slab_f, bias_row, mhg, c0, n0, m0)


def _sortable(x):
    x = jnp.where(x == 0.0, 0.0, x)
    bits = lax.bitcast_convert_type(x, I32)
    return bits ^ ((bits >> 31) & jnp.int32(0x7FFFFFFF))


def _count(keys_ref, ntiles, pred):
    shape = keys_ref.shape[1:]

    def body(j, acc):
        return acc + jnp.where(pred(j, keys_ref[j]), 1.0, 0.0)

    acc = lax.fori_loop(0, ntiles, body, jnp.zeros(shape, F32))
    return jnp.broadcast_to(jnp.sum(acc, axis=-1, keepdims=True), shape)


def _topk_select(keys_ref, ntiles, topk):
    shape = keys_ref.shape[1:]
    kf = float(topk)

    def bit_body(it, thr):
        cand = thr + lax.shift_left(jnp.int32(1), 31 - it)
        cnt = _count(keys_ref, ntiles, lambda j, x: x >= cand)
        return jnp.where(cnt >= kf, cand, thr)

    thr = lax.fori_loop(0, 32, bit_body, jnp.full(shape, INT_MIN, I32))
    n_gt = _count(keys_ref, ntiles, lambda j, x: x > thr)
    n_ge = _count(keys_ref, ntiles, lambda j, x: x >= thr)
    need = kf - n_gt
    lane = lax.broadcasted_iota(I32, shape, len(shape) - 1)
    nbits = max(1, int(keys_ref.shape[0] * LANE - 1).bit_length())

    def tie_fn():
        def pos_body(it, qp):
            cand = qp + lax.shift_left(jnp.int32(1), nbits - 1 - it)
            cnt = _count(keys_ref, ntiles, lambda j, x: (x == thr) & (j * LANE + lane < cand))
            return jnp.where(cnt <= need - 1.0, cand, qp)

        return lax.fori_loop(0, nbits, pos_body, jnp.zeros(shape, I32))

    def no_tie_fn():
        return jnp.full(shape, 2 ** nbits - 1, I32)

    has_tie = jnp.max(n_ge) > kf
    qpos = lax.cond(has_tie, tie_fn, no_tie_fn)
    return thr, qpos


def _selected(keys, j, thr, qpos):
    lane = lax.broadcasted_iota(I32, keys.shape, len(keys.shape) - 1)
    tie_ok = (keys == thr) & (j * LANE + lane <= qpos) & (keys > INT_MIN)
    return (keys > thr) | tie_ok


def _dsa_prompt_kernel(q_ref, iq_ref, g_ref, k_ref, v_ref, ki_ref, bias_ref, o_ref,
                       keys_scr, ki2_scr, wb_scr, logit_scr, m_scr, l_scr, acc_scr, *, topk):
    i = pl.program_id(1)
    ntiles = i + 2
    nt_all = keys_scr.shape[0]
    row = lax.broadcasted_iota(I32, (LANE, LANE), 0)
    lane = lax.broadcasted_iota(I32, (LANE, LANE), 1)
    qpos = N_META + LANE * i + row

    def phys_row(j):
        return pl.multiple_of(jnp.where(j == 0, META_TILE, j - 1) * LANE, LANE)

    @pl.when(i == 0)
    def _():
        z = jnp.zeros((LANE, IDX_D), F32)
        for j in range(nt_all):
            phys = META_TILE if j == 0 else j - 1
            kt = ki_ref[phys * LANE:(phys + 1) * LANE, SM_IK:SM_IK + IDX_D].astype(F32)
            both = jnp.concatenate([jnp.concatenate([kt, z], axis=1), jnp.concatenate([z, kt], axis=1)], axis=0)
            ki2_scr[j] = both.astype(BF16)

    w_all = g_ref[:, SM_IW:SM_IW + IDX_H] * (IDX_D ** -0.5 * IDX_H ** -0.5)
    for h in range(IDX_H):
        wb_scr[h] = jnp.broadcast_to(w_all[:, h:h + 1], (LANE, LANE))

    def score_body(j, carry):
        k2 = ki2_scr[j]
        sc = jnp.zeros((LANE, LANE), F32)
        for hp in range(IDX_H // 2):
            d2 = _dot_nt(iq_ref[:, hp * LANE:(hp + 1) * LANE], k2)
            sc = (sc + jnp.maximum(d2[:, :LANE], 0.0) * wb_scr[2 * hp]
                  + jnp.maximum(d2[:, LANE:], 0.0) * wb_scr[2 * hp + 1])
        kpos = N_META + LANE * (j - 1) + lane
        valid = (kpos >= 0) & (kpos <= qpos)
        keys_scr[j] = jnp.where(valid, _sortable(sc), INT_MIN)
        return carry

    lax.fori_loop(0, ntiles, score_body, 0)
    thr, tie_pos = _topk_select(keys_scr, ntiles, topk)

    m_scr[...] = jnp.full(m_scr.shape, NEG_BIG, F32)

    def qk_body(j, carry):
        r0 = phys_row(j)
        sel = _selected(keys_scr[j], j, thr, tie_pos)
        dsel = jnp.minimum(i - (j - 1), 2)
        for h in range(A_H):
            sl = slice(h * A_DH, (h + 1) * A_DH)
            s = _dot_nt(q_ref[:, sl], k_ref[pl.ds(r0, LANE), sl]) * (A_DH ** -0.5) + bias_ref[dsel, h]
            s = jnp.where(sel, s, NEG_BIG)
            logit_scr[h, j] = s
            m_scr[h] = jnp.maximum(m_scr[h], s)
        return carry

    lax.fori_loop(0, ntiles, qk_body, 0)
    for h in range(A_H):
        m_scr[h] = jnp.broadcast_to(jnp.max(m_scr[h], axis=-1, keepdims=True), (LANE, LANE))
    l_scr[...] = jnp.zeros_like(l_scr)
    acc_scr[...] = jnp.zeros_like(acc_scr)

    def pv_body(j, carry):
        r0 = phys_row(j)
        for h in range(A_H):
            sl = slice(h * A_DH, (h + 1) * A_DH)
            p = jnp.exp(logit_scr[h, j] - m_scr[h])
            l_scr[h] += p
            acc_scr[h] += _dot(p.astype(BF16), v_ref[pl.ds(r0, LANE), sl])
        return carry

    lax.fori_loop(0, ntiles, pv_body, 0)
    for h in range(A_H):
        l = jnp.sum(l_scr[h], axis=-1, keepdims=True)
        o_ref[:, h * A_DH:(h + 1) * A_DH] = (acc_scr[h] / l).astype(o_ref.dtype)


def _dsa_prompt(slab_b, slab_f, bias_tiles, topk):
    nb = slab_b.shape[0]
    nq = (T_PAD - LANE) // LANE
    wide = A_H * A_DH
    return pl.pallas_call(
        functools.partial(_dsa_prompt_kernel, topk=topk),
        grid=(nb, nq),
        in_specs=[pl.BlockSpec((None, LANE, wide), lambda b, i: (b, i, COL_AQ // wide)),
                  pl.BlockSpec((None, LANE, wide), lambda b, i: (b, i, COL_IQ // wide)),
                  pl.BlockSpec((None, LANE, LANE), lambda b, i: (b, i, COL_SM // LANE)),
                  _resident((None, T_PAD, wide), lambda b, i: (b, 0, COL_AK // wide)),
                  _resident((None, T_PAD, wide), lambda b, i: (b, 0, COL_AV // wide)),
                  _resident((None, T_PAD, LANE), lambda b, i: (b, 0, COL_SM // LANE)),
                  _resident((3, A_H, LANE, LANE), lambda b, i: (0, 0, 0, 0))],
        out_specs=pl.BlockSpec((None, LANE, wide), lambda b, i: (b, i, 0)),
        out_shape=jax.ShapeDtypeStruct((nb, T_PAD - LANE, wide), BF16),
        scratch_shapes=[pltpu.VMEM((nq + 1, LANE, LANE), I32),
                        pltpu.VMEM((nq + 1, 2 * LANE, 2 * IDX_D), BF16),
                        pltpu.VMEM((IDX_H, LANE, LANE), F32),
                        pltpu.VMEM((A_H, nq + 1, LANE, LANE), F32),
                        pltpu.VMEM((A_H, LANE, LANE), F32),
                        pltpu.VMEM((A_H, LANE, LANE), F32),
                        pltpu.VMEM((A_H, LANE, A_DH), F32)],
        compiler_params=_cparams(("arbitrary", "arbitrary")),
        name="dsa_prompt",
    )(slab_b, slab_b, slab_f, slab_b, slab_b, slab_b, bias_tiles)


def _dsa_sample_score_kernel(pt_ref, iq_ref, g_ref, *refs, seq, n_pages):
    page_refs, o_ref, kpad = refs[:n_pages], refs[n_pages], refs[n_pages + 1]

    @pl.when(pl.program_id(0) == 0)
    def _():
        kpad[...] = jnp.zeros_like(kpad)

    row = lax.broadcasted_iota(I32, (seq, LANE), 0)
    lane = lax.broadcasted_iota(I32, (seq, LANE), 1)
    w_all = g_ref[:, SM_IW:SM_IW + IDX_H] * (IDX_D ** -0.5 * IDX_H ** -0.5)
    wb = [jnp.broadcast_to(w_all[:, h:h + 1], (seq, LANE)) for h in range(IDX_H)]
    iqs = jnp.concatenate([iq_ref[:, h * IDX_D:(h + 1) * IDX_D] for h in range(IDX_H)], axis=0).astype(BF16)

    def scores(kt):
        d = _dot_nt(iqs, kt)
        sc = jnp.zeros((seq, LANE), F32)
        for h in range(IDX_H):
            sc = sc + jnp.maximum(d[h * seq:(h + 1) * seq], 0.0) * wb[h]
        return sc

    for p in range(n_pages):
        o_ref[p] = _sortable(scores(page_refs[p][...].astype(BF16)))
    kpad[0:seq, :] = g_ref[:, SM_IK:SM_IK + IDX_D]
    sc = scores(kpad[...].astype(BF16))
    o_ref[n_pages] = jnp.where(lane <= row, _sortable(sc), INT_MIN)


def _page_map(p):
    return lambda b, pt: (pt[b, p], 0, 0)


def _dsa_sample_scores(page_table, slab_f, kidx_pool, seq):
    nb, n_pages = page_table.shape
    page_specs = [pl.BlockSpec((None, PAGE, IDX_D), _page_map(p)) for p in range(n_pages)]
    grid_spec = pltpu.PrefetchScalarGridSpec(
        num_scalar_prefetch=1,
        grid=(nb,),
        in_specs=[pl.BlockSpec((seq, IDX_H * IDX_D), lambda b, pt: (b, COL_IQ // 1024)),
                  pl.BlockSpec((seq, LANE), lambda b, pt: (b, COL_SM // LANE))] + page_specs,
        out_specs=pl.BlockSpec((n_pages + 1, seq, LANE), lambda b, pt: (0, b, 0)),
        scratch_shapes=[pltpu.VMEM((LANE, IDX_D), F32)],
    )
    return pl.pallas_call(
        functools.partial(_dsa_sample_score_kernel, seq=seq, n_pages=n_pages),
        grid_spec=grid_spec,
        out_shape=jax.ShapeDtypeStruct((n_pages + 1, nb * seq, LANE), I32),
        compiler_params=_cparams(("arbitrary",)),
        name="dsa_sample_scores",
    )(page_table, slab_f, slab_f, *([kidx_pool] * n_pages))


def _dsa_select_kernel(keys_ref, thr_ref, pos_ref, *, topk):
    thr, tie_pos = _topk_select(keys_ref, keys_ref.shape[0], topk)
    thr_ref[...] = thr
    pos_ref[...] = tie_pos


def _dsa_select(keys, topk):
    nt, rows, _ = keys.shape
    return pl.pallas_call(
        functools.partial(_dsa_select_kernel, topk=topk),
        grid=(rows // LANE,),
        in_specs=[pl.BlockSpec((nt, LANE, LANE), lambda r: (0, r, 0))],
        out_specs=[pl.BlockSpec((LANE, LANE), lambda r: (r, 0)),
                   pl.BlockSpec((LANE, LANE), lambda r: (r, 0))],
        out_shape=[jax.ShapeDtypeStruct((rows, LANE), I32),
                   jax.ShapeDtypeStruct((rows, LANE), I32)],
        compiler_params=_cparams(("arbitrary",)),
        name="dsa_select",
    )(keys)


SA_PG = 4


def _dsa_sample_attend_kernel(pt_ref, q_ref, nk_ref, nv_ref, *refs, seq, n_pages):
    kp, vp = refs[:SA_PG], refs[SA_PG:2 * SA_PG]
    keys_ref, thr_ref, pos_ref, bias_ref, o_ref, s_scr, vb, kpad, vpad = refs[2 * SA_PG:]
    b, g = pl.program_id(0), pl.program_id(1)
    ntile = n_pages + 1

    @pl.when((b == 0) & (g == 0))
    def _():
        kpad[...] = jnp.zeros_like(kpad)
        vpad[...] = jnp.zeros_like(vpad)

    qh = [q_ref[:, h * A_DH:(h + 1) * A_DH].astype(BF16) for h in range(A_H)]
    for j in range(SA_PG):
        t = g * SA_PG + j
        r0 = pl.multiple_of(t * PAGE, PAGE)
        for h in range(A_H):
            kh = kp[j][pl.ds(h, PAGE, stride=A_H), :].astype(BF16)
            s_scr[h, t] = _dot_nt(qh[h], kh)
            vb[h, pl.ds(r0, PAGE), :] = vp[j][pl.ds(h, PAGE, stride=A_H), :].astype(BF16)

    @pl.when(g == pl.num_programs(1) - 1)
    def _():
        kpad[0:seq, :] = nk_ref[...]
        vpad[0:seq, :] = nv_ref[...]
        thr, tie_pos = thr_ref[...], pos_ref[...]
        sel = jnp.concatenate([_selected(keys_ref[t], t, thr, tie_pos) for t in range(ntile)], axis=1)
        for h in range(A_H):
            sl = slice(h * A_DH, (h + 1) * A_DH)
            s_scr[h, n_pages] = _dot_nt(qh[h], kpad[:, sl].astype(BF16))
            vb[h, n_pages * PAGE:ntile * PAGE, :] = vpad[:, sl].astype(BF16)
            s = jnp.concatenate([s_scr[h, t] for t in range(ntile)], axis=1) * (A_DH ** -0.5) + bias_ref[h]
            s = jnp.where(sel, s, NEG_BIG)
            p = jnp.where(sel, jnp.exp(s - jnp.max(s, axis=-1, keepdims=True)), 0.0)
            out = _dot(p.astype(BF16), vb[h]) / jnp.sum(p, axis=-1, keepdims=True)
            o_ref[:, sl] = out.astype(o_ref.dtype)


def _group_page_map(j):
    return lambda b, g, pt: (pt[b, g * SA_PG + j], 0, 0)


def _dsa_sample_attend(page_table, slab_f, k_pool, v_pool, keys, thr, tie_pos, bias_s, seq):
    nb, n_pages = page_table.shape
    assert n_pages % SA_PG == 0
    wide = A_H * A_DH
    ntile = n_pages + 1
    page_specs = [pl.BlockSpec((None, PAGE * A_H, A_DH), _group_page_map(j)) for j in range(SA_PG)]
    grid_spec = pltpu.PrefetchScalarGridSpec(
        num_scalar_prefetch=1,
        grid=(nb, n_pages // SA_PG),
        in_specs=[pl.BlockSpec((seq, wide), lambda b, g, pt: (b, COL_AQ // wide)),
                  pl.BlockSpec((seq, wide), lambda b, g, pt: (b, COL_AK // wide)),
                  pl.BlockSpec((seq, wide), lambda b, g, pt: (b, COL_AV // wide))]
                 + page_specs + page_specs
                 + [pl.BlockSpec((ntile, seq, LANE), lambda b, g, pt: (0, b, 0)),
                    pl.BlockSpec((seq, LANE), lambda b, g, pt: (b, 0)),
                    pl.BlockSpec((seq, LANE), lambda b, g, pt: (b, 0)),
                    _resident((A_H, seq, ntile * LANE), lambda b, g, pt: (0, 0, 0))],
        out_specs=pl.BlockSpec((seq, wide), lambda b, g, pt: (b, 0)),
        scratch_shapes=[pltpu.VMEM((A_H, ntile, seq, LANE), F32),
                        pltpu.VMEM((A_H, ntile * PAGE, A_DH), BF16),
                        pltpu.VMEM((PAGE, wide), F32),
                        pltpu.VMEM((PAGE, wide), F32)],
    )
    return pl.pallas_call(
        functools.partial(_dsa_sample_attend_kernel, seq=seq, n_pages=n_pages),
        grid_spec=grid_spec,
        out_shape=jax.ShapeDtypeStruct((nb * seq, wide), F32),
        compiler_params=_cparams(("arbitrary", "arbitrary")),
        name="dsa_sample_attend",
    )(page_table, slab_f, slab_f, slab_f, *([k_pool] * SA_PG), *([v_pool] * SA_PG), keys, thr, tie_pos, bias_s)


MO_TM = 256


def _mixout_kernel(mm_ref, ma_ref, x_ref, wo_ref, g_ref, wq_ref, k1_ref, k2_ref,
                   hp_ref, xnt_ref, s1_ref, s2_ref):
    half = M_H * M_DV
    y = (_dot(mm_ref[...].astype(BF16), wo_ref[0:half, :])
         + _dot(ma_ref[...].astype(BF16), wo_ref[half:, :]))
    hp = x_ref[...] + y
    hp_ref[...] = hp
    xn = hp * lax.rsqrt(jnp.mean(hp * hp, axis=-1, keepdims=True) + RMS_EPS) * g_ref[...]
    xnt = xn.T.astype(BF16)
    xnt_ref[...] = xnt
    qqt = _dot(wq_ref[...], xnt)
    hq = PEER_DQ // 2
    for h in range(PEER_H):
        q1 = qqt[h * PEER_DQ:h * PEER_DQ + hq, :].astype(BF16)
        q2 = qqt[h * PEER_DQ + hq:(h + 1) * PEER_DQ, :].astype(BF16)
        s1_ref[h] = _dot(k1_ref[h], q1)
        s2_ref[h] = _dot(k2_ref[h], q2)


def _mixout(mm, ma, x2d, wo, g2, wqt, k1, k2):
    rows = x2d.shape[0]
    half = M_H * M_DV
    c2 = lambda i: (0, 0)
    c3 = lambda i: (0, 0, 0)
    return pl.pallas_call(
        _mixout_kernel,
        grid=(rows // MO_TM,),
        in_specs=[pl.BlockSpec((MO_TM, half), lambda i: (i, 0)),
                  pl.BlockSpec((MO_TM, A_H * A_DH), lambda i: (i, 0)),
                  pl.BlockSpec((MO_TM, D_MODEL), lambda i: (i, 0)),
                  _resident((half + A_H * A_DH, D_MODEL), c2),
                  _resident((1, D_MODEL), c2),
                  _resident((PEER_H * PEER_DQ, D_MODEL), c2),
                  _resident((PEER_H, N_KEYS, PEER_DQ // 2), c3),
                  _resident((PEER_H, N_KEYS, PEER_DQ // 2), c3)],
        out_specs=[pl.BlockSpec((MO_TM, D_MODEL), lambda i: (i, 0)),
                   pl.BlockSpec((D_MODEL, MO_TM), lambda i: (0, i)),
                   pl.BlockSpec((PEER_H, N_KEYS, MO_TM), lambda i: (0, 0, i)),
                   pl.BlockSpec((PEER_H, N_KEYS, MO_TM), lambda i: (0, 0, i))],
        out_shape=[jax.ShapeDtypeStruct((rows, D_MODEL), F32),
                   jax.ShapeDtypeStruct((D_MODEL, rows), BF16),
                   jax.ShapeDtypeStruct((PEER_H, N_KEYS, rows), F32),
                   jax.ShapeDtypeStruct((PEER_H, N_KEYS, rows), F32)],
        compiler_params=_cparams(("arbitrary",)),
        name="mixout",
    )(mm, ma, x2d, wo, g2, wqt, k1, k2)


PS_TN = 256


def _extract_top(s, count):
    n = s.shape[0]
    idx = lax.broadcasted_iota(I32, s.shape, 0)
    rank = jnp.full(s.shape, float(count), F32)
    rem = s
    vals = []
    for r in range(count):
        mx = jnp.max(rem, axis=0, keepdims=True)
        first = jnp.min(jnp.where(rem == mx, idx, n), axis=0, keepdims=True)
        hit = idx == first
        rank = jnp.where(hit, float(r), rank)
        rem = jnp.where(hit, -jnp.inf, rem)
        vals.append(mx)
    return vals, rank


def _peer_select_kernel(s1_ref, s2_ref, li_ref, r2_ref, c1_ref, e2_ref):
    s1 = s1_ref[...]
    s2 = s2_ref[...]
    v1, rank1 = _extract_top(s1, PEER_TOPK)
    v2, rank2 = _extract_top(s2, PEER_TOPK)
    r16 = lax.broadcasted_iota(I32, (PEER_TOPK, s2.shape[1]), 0)
    v2_all = jnp.zeros((PEER_TOPK, s2.shape[1]), F32)
    for r in range(PEER_TOPK):
        v2_all = jnp.where(r16 == r, v2[r], v2_all)
    cand = jnp.concatenate([v1[r] + v2_all for r in range(PEER_TOPK)], axis=0)
    sc, crank = _extract_top(cand, PEER_TOPK)
    chosen = jnp.where(crank < float(PEER_TOPK), 1.0, 0.0)
    li = jnp.zeros(s1.shape, F32)
    for r in range(PEER_TOPK):
        cnt = jnp.sum(chosen[r * PEER_TOPK:(r + 1) * PEER_TOPK], axis=0, keepdims=True)
        li = li + jnp.where(rank1 == float(r), cnt, 0.0)
    z = jnp.zeros_like(sc[0])
    for c in range(PEER_TOPK):
        z = z + jnp.exp(sc[c] - sc[0])
    li_ref[...] = li
    r2_ref[...] = rank2
    c1_ref[...] = jnp.exp(s1 - v1[0]) / z
    e2_ref[...] = jnp.exp(s2 - v2[0])


def _peer_select(s1t, s2t):
    nh, nk, rows = s1t.shape
    spec = pl.BlockSpec((None, nk, PS_TN), lambda t, h: (h, 0, t))
    shp = jax.ShapeDtypeStruct((nh, nk, rows), F32)
    return pl.pallas_call(
        _peer_select_kernel,
        grid=(rows // PS_TN, nh),
        in_specs=[spec, spec],
        out_specs=[spec, spec, spec, spec],
        out_shape=[shp, shp, shp, shp],
        compiler_params=_cparams(("arbitrary", "arbitrary")),
        name="peer_select",
    )(s1t, s2t)


PD_TN = 512
PD_NI = 8
PD_EB = PD_NI * N_KEYS


def _peer_dense_kernel(xnt_ref, u_ref, vt_ref, li_ref, r2_ref, c1_ref, e2_ref, o_ref,
                       acc_scr, a_scr, p_scr):
    e = pl.program_id(1)

    @pl.when(e == 0)
    def _():
        acc_scr[...] = jnp.zeros_like(acc_scr)

    a_scr[...] = _dot(u_ref[...], xnt_ref[...])
    for il in range(PD_NI):
        rows = slice(il * N_KEYS, (il + 1) * N_KEYS)
        for tc in range(PD_TN // LANE):
            cols = slice(tc * LANE, (tc + 1) * LANE)
            w = jnp.zeros((N_KEYS, LANE), F32)
            for h in range(PEER_H):
                li = li_ref[h, il:il + 1, cols]
                c1 = c1_ref[h, il:il + 1, cols]
                w = w + jnp.where(r2_ref[h, :, cols] < li, e2_ref[h, :, cols], 0.0) * c1
            a = a_scr[rows, cols]
            act = 0.5 * a * (1.0 + lax.erf(a * (2.0 ** -0.5)))
            p_scr[rows, cols] = (w * act).astype(BF16)
    acc_scr[...] += _dot(vt_ref[...], p_scr[...])

    @pl.when(e == pl.num_programs(1) - 1)
    def _():
        o_ref[...] = acc_scr[...].T


def _peer_dense(xnt, u, vt, li, r2, c1, e2):
    rows = xnt.shape[1]
    key2_spec = pl.BlockSpec((PEER_H, N_KEYS, PD_TN), lambda t, e: (0, 0, t))
    key1_spec = pl.BlockSpec((PEER_H, PD_NI, PD_TN), lambda t, e: (0, e, t))
    return pl.pallas_call(
        _peer_dense_kernel,
        grid=(rows // PD_TN, N_EXPERTS // PD_EB),
        in_specs=[pl.BlockSpec((D_MODEL, PD_TN), lambda t, e: (0, t)),
                  pl.BlockSpec((PD_EB, D_MODEL), lambda t, e: (e, 0)),
                  pl.BlockSpec((D_MODEL, PD_EB), lambda t, e: (0, e)),
                  key1_spec, key2_spec, key1_spec, key2_spec],
        out_specs=pl.BlockSpec((PD_TN, D_MODEL), lambda t, e: (t, 0)),
        out_shape=jax.ShapeDtypeStruct((rows, D_MODEL), F32),
        scratch_shapes=[pltpu.VMEM((D_MODEL, PD_TN), F32),
                        pltpu.VMEM((PD_EB, PD_TN), F32),
                        pltpu.VMEM((PD_EB, PD_TN), BF16)],
        compiler_params=_cparams(("arbitrary", "arbitrary")),
        name="peer_dense",
    )(xnt, u, vt, li, r2, c1, e2)


def _final_kernel(hp_ref, po_ref, g_ref, y_ref):
    h = hp_ref[...] + po_ref[...]
    y_ref[...] = h * lax.rsqrt(jnp.mean(h * h, axis=-1, keepdims=True) + RMS_EPS) * g_ref[...]


def _final(hp, po, g):
    rows = hp.shape[0]
    spec = pl.BlockSpec((MO_TM, D_MODEL), lambda i: (i, 0))
    return pl.pallas_call(
        _final_kernel,
        grid=(rows // MO_TM,),
        in_specs=[spec, spec, pl.BlockSpec((1, D_MODEL), lambda i: (0, 0))],
        out_specs=spec,
        out_shape=jax.ShapeDtypeStruct((rows, D_MODEL), F32),
        compiler_params=_cparams(("arbitrary",)),
        name="final_norm",
    )(hp, po, g)


def _t5_bucket_table(n):
    rel = np.arange(n)
    max_exact = REL_BUCKETS // 2
    nf = np.maximum(rel, 1).astype(np.float32)
    large = max_exact + (np.log(nf / np.float32(max_exact)) / np.float32(math.log(REL_MAX_DIST / max_exact))
                         * np.float32(REL_BUCKETS - max_exact)).astype(np.int32)
    large = np.minimum(large, REL_BUCKETS - 1)
    return np.where(rel < max_exact, rel, large).astype(np.int32)


def _bias_tiles(rel_bias):
    table = _t5_bucket_table(3 * LANE)
    r = np.arange(LANE)[:, None]
    c = np.arange(LANE)[None, :]
    rel = np.stack([np.maximum(r - c, 0), LANE + r - c, np.full((LANE, LANE), 2 * LANE)])
    assert table[LANE + 1] == REL_BUCKETS - 1
    buckets = table[rel]
    return jnp.transpose(rel_bias.astype(F32)[buckets], (0, 3, 1, 2))


def _sample_bias(rel_bias, past, seq, ntile):
    t = np.arange(seq)[:, None]
    c = np.arange(ntile * LANE)[None, :]
    buckets = _t5_bucket_table(past + seq)[np.maximum(past + t - c, 0)]
    return jnp.transpose(rel_bias.astype(F32)[buckets], (2, 0, 1))


def _relayout_w_in(w):
    cuts = np.cumsum(IN_SPLITS)[:-1].tolist()
    mq, mk, mv, mo, ig, fg, aq, ak, av, iq, ik, iw = jnp.split(w, cuts, axis=-1)
    pad = jnp.zeros((w.shape[0], P_PAD - COL_SM - IDX_D - IDX_H - 2 * M_H), w.dtype)
    return jnp.concatenate([aq, ak, av, iq, mv, mo, mq, mk, ik, iw, ig, fg, pad], axis=-1).astype(BF16)


def kernel(x_prompt, x_sample, cache_k, cache_v, cache_kidx, state_C, state_n, state_m, page_table,
           meta_tokens, rel_bias, final_norm_g, norm1_g, w_in, b_i, b_f, mh_norm_g, w_out, norm2_g,
           peer_wq, peer_k1, peer_k2, peer_u, peer_v):
    B, S_p, D = x_prompt.shape
    Bs, Ts, _ = x_sample.shape
    n_pages = page_table.shape[1]
    depth = w_in.shape[0]
    assert depth == 1 and D == D_MODEL and N_META + S_p == T_PAD - META_OFF and S_p % LANE == 0
    assert n_pages * PAGE + Ts <= (n_pages + 1) * PAGE and cache_k.shape[2] == PAGE
    topk_p = min(TOPK_MAX, S_p // 4)
    topk_s = min(TOPK_MAX, (n_pages * PAGE + Ts) // 4)
    l = 0

    w1 = _relayout_w_in(w_in[l])
    g1 = norm1_g[l].astype(F32).reshape(1, D)
    g2 = norm2_g[l].astype(F32).reshape(1, D)
    gf = final_norm_g.astype(F32).reshape(1, D)
    zpad = jnp.zeros((LANE - SM_FG - M_H,), F32)
    bias_row = jnp.concatenate([jnp.zeros((SM_IG,), F32), b_i[l].astype(F32), b_f[l].astype(F32), zpad]).reshape(1, LANE)
    mhg = mh_norm_g[l].astype(F32).reshape(1, M_H * M_DV)
    wo = w_out[l].astype(BF16)
    wqt = peer_wq[l].T.astype(BF16)
    k1 = peer_k1[l].astype(BF16)
    k2 = peer_k2[l].astype(BF16)
    u = peer_u[l].astype(BF16)
    vt = peer_v[l].T.astype(BF16)
    bias_tiles = _bias_tiles(rel_bias)

    meta = jnp.broadcast_to(meta_tokens.astype(F32)[None], (B, N_META, D))
    xp_all = jnp.concatenate([x_prompt, jnp.zeros((B, META_OFF, D), F32), meta], axis=1)

    pf, pb = _inproj(xp_all.reshape(B * T_PAD, D), g1, w1)
    pf = pf.reshape(B, T_PAD, P_PAD)
    pb = pb.reshape(B, T_PAD, P_PAD)
    mm_p, c_p, n_p, m_p = _mlstm_prompt(pb, pf, bias_row, mhg)
    ma_p = _dsa_prompt(pb, pf, bias_tiles, topk_p)
    xp2 = x_prompt.reshape(B * S_p, D)
    hp_p, xnt_p, s1_p, s2_p = _mixout(mm_p.reshape(B * S_p, -1), ma_p.reshape(B * S_p, -1), xp2, wo, g2, wqt, k1, k2)
    po_p = _peer_dense(xnt_p, u, vt, *_peer_select(s1_p, s2_p))
    y_prompt = _final(hp_p, po_p, gf).reshape(B, S_p, D)

    xs2 = x_sample.reshape(Bs * Ts, D)
    sf, _ = _inproj(xs2, g1, w1)
    c0 = state_C[l].astype(F32)
    n0 = state_n[l].astype(F32).reshape(Bs, M_H, 1, M_DK)
    m0 = jnp.broadcast_to(state_m[l].astype(F32)[:, :, None, None], (Bs, M_H, 1, LANE))
    mm_s, c_s, n_s, m_s = _mlstm_sample(sf, bias_row, mhg, c0, n0, m0, Ts)
    n_pool = cache_k.shape[1]
    keys_s = _dsa_sample_scores(page_table, sf, cache_kidx[l].reshape(n_pool, PAGE, IDX_D), Ts)
    thr_s, pos_s = _dsa_select(keys_s, topk_s)
    bias_s = _sample_bias(rel_bias, n_pages * PAGE, Ts, n_pages + 1)
    ma_s = _dsa_sample_attend(page_table, sf, cache_k[l].reshape(n_pool, PAGE * A_H, A_DH),
                              cache_v[l].reshape(n_pool, PAGE * A_H, A_DH), keys_s, thr_s, pos_s, bias_s, Ts)
    hp_s, xnt_s, s1_s, s2_s = _mixout(mm_s, ma_s, xs2, wo, g2, wqt, k1, k2)
    po_s = _peer_dense(xnt_s, u, vt, *_peer_select(s1_s, s2_s))
    y_sample = _final(hp_s, po_s, gf).reshape(Bs, Ts, D)

    def seq_rows(a):
        return jnp.concatenate([a[:, T_PAD - N_META:], a[:, :S_p]], axis=1)

    wide = A_H * A_DH
    T = N_META + S_p
    k_prompt = seq_rows(pf[:, :, COL_AK:COL_AK + wide]).reshape(1, B, T, A_H, A_DH)
    v_prompt = seq_rows(pf[:, :, COL_AV:COL_AV + wide]).reshape(1, B, T, A_H, A_DH)
    kidx_prompt = seq_rows(pf[:, :, COL_SM + SM_IK:COL_SM + SM_IK + IDX_D]).reshape(1, B, T, IDX_D)
    k_sample = sf[:, COL_AK:COL_AK + wide].reshape(1, Bs, Ts, A_H, A_DH)
    v_sample = sf[:, COL_AV:COL_AV + wide].reshape(1, Bs, Ts, A_H, A_DH)
    kidx_sample = sf[:, COL_SM + SM_IK:COL_SM + SM_IK + IDX_D].reshape(1, Bs, Ts, IDX_D)
    return (y_prompt, y_sample,
            k_prompt, v_prompt, kidx_prompt,
            c_p[None], n_p[:, :, 0, :][None], m_p[:, :, 0, 0][None],
            k_sample, v_sample, kidx_sample,
            c_s[None], n_s[:, :, 0, :][None], m_s[:, :, 0, 0][None])
```

```python
import functools
import math

import numpy as np
import jax
import jax.numpy as jnp
from jax import lax
from jax.experimental import pallas as pl
from jax.experimental.pallas import tpu as pltpu

F32 = jnp.float32
BF16 = jnp.bfloat16
I32 = jnp.int32

D_MODEL = 2048
N_META = 16
RMS_EPS = 1e-6
M_H, M_DK, M_DV = 4, 128, 256
A_H, A_DH = 8, 128
IDX_H, IDX_D = 16, 64
TOPK_MAX = 256
REL_BUCKETS, REL_MAX_DIST = 32, 128
PEER_H, PEER_DQ, N_KEYS, PEER_TOPK = 8, 256, 128, 16
N_EXPERTS = N_KEYS * N_KEYS
PAGE = 128
IN_SPLITS = (M_H * M_DK, M_H * M_DK, M_H * M_DV, M_H * M_DV, M_H, M_H,
             A_H * A_DH, A_H * A_DH, A_H * A_DH, IDX_H * IDX_D, IDX_D, IDX_H)

LANE = 128
VMEM_LIMIT = 56 * 1024 * 1024

COL_AQ, COL_AK, COL_AV, COL_IQ, COL_MV, COL_MO, COL_MQ, COL_MK, COL_SM = (
    0, 1024, 2048, 3072, 4096, 5120, 6144, 6656, 7168)
P_PAD = COL_SM + 512
SM_IK, SM_IW, SM_IG, SM_FG = 0, 64, 80, 84
T_PAD = 17 * LANE
META_TILE = 16
META_OFF = LANE - N_META
INT_MIN = -(2 ** 31)
NEG_BIG = -1e30


def _cparams(sem, vmem=VMEM_LIMIT):
    return pltpu.CompilerParams(dimension_semantics=sem, vmem_limit_bytes=vmem)


def _resident(shape, index_map):
    return pl.BlockSpec(shape, index_map, pipeline_mode=pl.Buffered(1))


def _dot(a, b):
    return jnp.dot(a, b, preferred_element_type=F32)


def _dot_nt(a, b):
    return lax.dot_general(a, b, (((1,), (1,)), ((), ())), preferred_element_type=F32)


def _dot_tn(a, b):
    return lax.dot_general(a, b, (((0,), (0,)), ((), ())), preferred_element_type=F32)


IP_TM_MAX, IP_TN = 1088, 512


def _inproj_kernel(x_ref, g_ref, w_ref, of_ref, ob_ref, hn_ref):
    @pl.when(pl.program_id(1) == 0)
    def _():
        x = x_ref[...]
        y = x * lax.rsqrt(jnp.mean(x * x, axis=-1, keepdims=True) + RMS_EPS) * g_ref[...]
        hn_ref[...] = y.astype(BF16)

    acc = _dot(hn_ref[...], w_ref[...])
    of_ref[...] = acc
    ob_ref[...] = acc.astype(BF16)


def _inproj(x2d, g, w):
    rows = x2d.shape[0]
    tm = IP_TM_MAX if rows % IP_TM_MAX == 0 else min(rows, 1024)
    assert rows % tm == 0
    grid = (rows // tm, P_PAD // IP_TN)
    return pl.pallas_call(
        _inproj_kernel,
        grid=grid,
        in_specs=[pl.BlockSpec((tm, D_MODEL), lambda i, j: (i, 0)),
                  pl.BlockSpec((1, D_MODEL), lambda i, j: (0, 0)),
                  pl.BlockSpec((D_MODEL, IP_TN), lambda i, j: (0, j))],
        out_specs=[pl.BlockSpec((tm, IP_TN), lambda i, j: (i, j)),
                   pl.BlockSpec((tm, IP_TN), lambda i, j: (i, j))],
        out_shape=[jax.ShapeDtypeStruct((rows, P_PAD), F32),
                   jax.ShapeDtypeStruct((rows, P_PAD), BF16)],
        scratch_shapes=[pltpu.VMEM((tm, D_MODEL), BF16)],
        compiler_params=_cparams(("arbitrary", "arbitrary")),
        name="inproj",
    )(x2d, g, w)


def _mlstm_gates(gates, bias_row, lo, hi):
    row = lax.broadcasted_iota(I32, (LANE, LANE), 0)
    col = lax.broadcasted_iota(I32, (LANE, LANE), 1)
    valid = (row >= lo) & (row < hi)
    pre = gates + bias_row
    li = jnp.where(valid, pre, -jnp.inf)
    lf = jnp.where(valid, jax.nn.log_sigmoid(pre), 0.0)
    tril = jnp.where(col <= row, 1.0, 0.0).astype(F32)
    b = jnp.dot(tril, lf, preferred_element_type=F32, precision=lax.Precision.HIGHEST)
    return li, b, li.T, b.T


def _mlstm_head(q, k, v, li, b, lit, bt, h, c_old, n_old, m_old):
    li_col = li[:, SM_IG + h:SM_IG + h + 1]
    b_col = b[:, SM_FG + h:SM_FG + h + 1]
    li_row = lit[SM_IG + h:SM_IG + h + 1, :]
    b_row = bt[SM_FG + h:SM_FG + h + 1, :]
    row = lax.broadcasted_iota(I32, (LANE, LANE), 0)
    col = lax.broadcasted_iota(I32, (LANE, LANE), 1)
    dmat = jnp.where(col <= row, b_col - b_row + li_row, -jnp.inf)
    inter = b_col + m_old
    m_t = jnp.maximum(inter, jnp.max(dmat, axis=-1, keepdims=True))
    w_inter = jnp.exp(inter - m_t)
    scale = M_DK ** -0.5
    s = _dot_nt(q, k) * scale * jnp.exp(dmat - m_t)
    num = w_inter * _dot(q, c_old.astype(BF16)) + _dot(s.astype(BF16), v)
    qf = q.astype(F32)
    den = w_inter * jnp.sum(qf * n_old, axis=-1, keepdims=True) + jnp.sum(s, axis=-1, keepdims=True)
    h_out = num / jnp.maximum(jnp.abs(den), jnp.exp(-m_t))
    b_last = b_col[LANE - 1:LANE, :]
    g_col = b_last - b_col + li_col
    m_new = jnp.maximum(b_last + m_old, jnp.max(g_col, axis=0, keepdims=True))
    a = jnp.exp(b_last + m_old - m_new)
    wk = jnp.exp(g_col - m_new)
    wv = (wk * v.astype(F32)).astype(BF16)
    c_new = a * c_old + scale * _dot_tn(k, wv)
    n_new = a * n_old + scale * jnp.sum(wk * k.astype(F32), axis=0, keepdims=True)
    return h_out, c_new, n_new, m_new


def _mlstm_mix(h_out, mo, g):
    hn = h_out * lax.rsqrt(jnp.mean(h_out * h_out, axis=-1, keepdims=True) + RMS_EPS)
    return hn * g * jax.nn.sigmoid(mo)


def _mlstm_prompt_kernel(q_ref, k_ref, v_ref, mo_ref, g_ref, bias_ref, mhg_ref,
                         out_ref, c_out, n_out, m_out, c_scr, n_scr, m_scr):
    c = pl.program_id(1)

    @pl.when(c == 0)
    def _():
        c_scr[...] = jnp.zeros_like(c_scr)
        n_scr[...] = jnp.zeros_like(n_scr)
        m_scr[...] = jnp.zeros_like(m_scr)

    lo = jnp.where(c == 0, META_OFF, 0)
    li, b, lit, bt = _mlstm_gates(g_ref[...], bias_ref[...], lo, LANE)
    for h in range(M_H):
        q = q_ref[:, h * M_DK:(h + 1) * M_DK]
        k = k_ref[:, h * M_DK:(h + 1) * M_DK]
        v = v_ref[:, h * M_DV:(h + 1) * M_DV]
        h_out, c_new, n_new, m_new = _mlstm_head(
            q, k, v, li, b, lit, bt, h, c_scr[h], n_scr[h], m_scr[h][:, 0:1])
        c_scr[h] = c_new
        n_scr[h] = n_new
        m_scr[h] = jnp.broadcast_to(m_new, (1, LANE))
        sl = slice(h * M_DV, (h + 1) * M_DV)
        out_ref[:, sl] = _mlstm_mix(h_out, mo_ref[:, sl], mhg_ref[:, sl]).astype(out_ref.dtype)

    @pl.when(c == pl.num_programs(1) - 1)
    def _():
        c_out[...] = c_scr[...]
        n_out[...] = n_scr[...]
        m_out[...] = m_scr[...]


def _mlstm_prompt(slab_b, slab_f, bias_row, mhg):
    nb = slab_b.shape[0]
    nchunk = T_PAD // LANE

    def tile(c):
        return (c + META_TILE) % nchunk

    return pl.pallas_call(
        _mlstm_prompt_kernel,
        grid=(nb, nchunk),
        in_specs=[pl.BlockSpec((None, LANE, M_H * M_DK), lambda b, c: (b, tile(c), COL_MQ // 512)),
                  pl.BlockSpec((None, LANE, M_H * M_DK), lambda b, c: (b, tile(c), COL_MK // 512)),
                  pl.BlockSpec((None, LANE, M_H * M_DV), lambda b, c: (b, tile(c), COL_MV // 1024)),
                  pl.BlockSpec((None, LANE, M_H * M_DV), lambda b, c: (b, tile(c), COL_MO // 1024)),
                  pl.BlockSpec((None, LANE, LANE), lambda b, c: (b, tile(c), COL_SM // LANE)),
                  pl.BlockSpec((1, LANE), lambda b, c: (0, 0)),
                  pl.BlockSpec((1, M_H * M_DV), lambda b, c: (0, 0))],
        out_specs=[pl.BlockSpec((None, LANE, M_H * M_DV), lambda b, c: (b, jnp.maximum(c - 1, 0), 0)),
                   pl.BlockSpec((None, M_H, M_DK, M_DV), lambda b, c: (b, 0, 0, 0)),
                   pl.BlockSpec((None, M_H, 1, M_DK), lambda b, c: (b, 0, 0, 0)),
                   pl.BlockSpec((None, M_H, 1, LANE), lambda b, c: (b, 0, 0, 0))],
        out_shape=[jax.ShapeDtypeStruct((nb, T_PAD - LANE, M_H * M_DV), BF16),
                   jax.ShapeDtypeStruct((nb, M_H, M_DK, M_DV), F32),
                   jax.ShapeDtypeStruct((nb, M_H, 1, M_DK), F32),
                   jax.ShapeDtypeStruct((nb, M_H, 1, LANE), F32)],
        scratch_shapes=[pltpu.VMEM((M_H, M_DK, M_DV), F32),
                        pltpu.VMEM((M_H, 1, M_DK), F32),
                        pltpu.VMEM((M_H, 1, LANE), F32)],
        compiler_params=_cparams(("arbitrary", "arbitrary")),
        name="mlstm_prompt",
    )(slab_b, slab_b, slab_b, slab_f, slab_f, bias_row, mhg)


MS_NB = 2


def _mlstm_sample_kernel(q_ref, k_ref, v_ref, mo_ref, g_ref, bias_ref, mhg_ref, c_in, n_in, m_in,
                         out_ref, c_out, n_out, m_out, qp, kp, vp, gp, *, seq):
    @pl.when(pl.program_id(0) == 0)
    def _():
        qp[...] = jnp.zeros_like(qp)
        kp[...] = jnp.zeros_like(kp)
        vp[...] = jnp.zeros_like(vp)
        gp[...] = jnp.zeros_like(gp)

    for s in range(MS_NB):
        rows = slice(s * seq, (s + 1) * seq)
        qp[0:seq, :] = q_ref[rows, :]
        kp[0:seq, :] = k_ref[rows, :]
        vp[0:seq, :] = v_ref[rows, :]
        gp[0:seq, :] = g_ref[rows, :]
        li, b, lit, bt = _mlstm_gates(gp[...], bias_ref[...], 0, seq)
        for h in range(M_H):
            q = qp[:, h * M_DK:(h + 1) * M_DK].astype(BF16)
            k = kp[:, h * M_DK:(h + 1) * M_DK].astype(BF16)
            v = vp[:, h * M_DV:(h + 1) * M_DV].astype(BF16)
            h_out, c_new, n_new, m_new = _mlstm_head(
                q, k, v, li, b, lit, bt, h, c_in[s, h], n_in[s, h], m_in[s, h][:, 0:1])
            c_out[s, h] = c_new
            n_out[s, h] = n_new
            m_out[s, h] = jnp.broadcast_to(m_new, (1, LANE))
            sl = slice(h * M_DV, (h + 1) * M_DV)
            mixed = _mlstm_mix(h_out[0:seq], mo_ref[rows, sl], mhg_ref[:, sl])
            out_ref[rows, sl] = mixed.astype(out_ref.dtype)


def _mlstm_sample(slab_f, bias_row, mhg, c0, n0, m0, seq):
    nb = c0.shape[0]
    rb = MS_NB * seq
    st4 = lambda i: (i, 0, 0, 0)
    return pl.pallas_call(
        functools.partial(_mlstm_sample_kernel, seq=seq),
        grid=(nb // MS_NB,),
        in_specs=[pl.BlockSpec((rb, M_H * M_DK), lambda i: (i, COL_MQ // 512)),
                  pl.BlockSpec((rb, M_H * M_DK), lambda i: (i, COL_MK // 512)),
                  pl.BlockSpec((rb, M_H * M_DV), lambda i: (i, COL_MV // 1024)),
                  pl.BlockSpec((rb, M_H * M_DV), lambda i: (i, COL_MO // 1024)),
                  pl.BlockSpec((rb, LANE), lambda i: (i, COL_SM // LANE)),
                  pl.BlockSpec((1, LANE), lambda i: (0, 0)),
                  pl.BlockSpec((1, M_H * M_DV), lambda i: (0, 0)),
                  pl.BlockSpec((MS_NB, M_H, M_DK, M_DV), st4),
                  pl.BlockSpec((MS_NB, M_H, 1, M_DK), st4),
                  pl.BlockSpec((MS_NB, M_H, 1, LANE), st4)],
        out_specs=[pl.BlockSpec((rb, M_H * M_DV), lambda i: (i, 0)),
                   pl.BlockSpec((MS_NB, M_H, M_DK, M_DV), st4),
                   pl.BlockSpec((MS_NB, M_H, 1, M_DK), st4),
                   pl.BlockSpec((MS_NB, M_H, 1, LANE), st4)],
        out_shape=[jax.ShapeDtypeStruct((nb * seq, M_H * M_DV), BF16),
                   jax.ShapeDtypeStruct((nb, M_H, M_DK, M_DV), F32),
                   jax.ShapeDtypeStruct((nb, M_H, 1, M_DK), F32),
                   jax.ShapeDtypeStruct((nb, M_H, 1, LANE), F32)],
        scratch_shapes=[pltpu.VMEM((LANE, M_H * M_DK), F32),
                        pltpu.VMEM((LANE, M_H * M_DK), F32),
                        pltpu.VMEM((LANE, M_H * M_DV), F32),
                        pltpu.VMEM((LANE, LANE), F32)],
        compiler_params=_cparams(("arbitrary",)),
        name="mlstm_sample",
    )(slab_f, slab_f, slab_f, slab_f, slab_f, bias_row, mhg, c0, n0, m0)


def _sortable(x):
    x = jnp.where(x == 0.0, 0.0, x)
    bits = lax.bitcast_convert_type(x, I32)
    return bits ^ ((bits >> 31) & jnp.int32(0x7FFFFFFF))


def _unsortable(key):
    return lax.bitcast_convert_type(key ^ ((key >> 31) & jnp.int32(0x7FFFFFFF)), F32)


def _row_reduce(s_ref, ntiles, fn, init, reduce):
    shape = s_ref.shape[1:]
    acc = lax.fori_loop(0, ntiles, lambda j, a: fn(a, j, s_ref[j]), jnp.full(shape, init, F32))
    return jnp.broadcast_to(reduce(acc, axis=-1, keepdims=True), shape)


def _count(s_ref, ntiles, pred):
    return _row_reduce(s_ref, ntiles, lambda a, j, x: a + jnp.where(pred(j, x), 1.0, 0.0), 0.0, jnp.sum)


TOPK_BISECTIONS = 32


def _topk_select(s_ref, ntiles, topk):
    shape = s_ref.shape[1:]
    inf = jnp.inf
    vmax = _row_reduce(s_ref, ntiles, lambda a, j, x: jnp.maximum(a, x), -inf, jnp.max)
    vmin = _row_reduce(s_ref, ntiles, lambda a, j, x: jnp.minimum(a, jnp.where(x > -inf, x, inf)), inf, jnp.min)
    n_valid = _count(s_ref, ntiles, lambda j, x: x > -inf)
    k = jnp.minimum(float(topk), n_valid)

    def in_bracket(x, lo, hi):
        return (x >= lo) & (x < hi) & (x > -inf)

    top_tied = _count(s_ref, ntiles, lambda j, x: x >= vmax) >= k
    lo = jnp.where(top_tied, vmax, vmin)
    hi = jnp.where(top_tied, inf, vmax)

    def halve(_, c):
        lo, hi = c
        mid = lo * 0.5 + hi * 0.5
        ge = _count(s_ref, ntiles, lambda j, x: x >= mid) >= k
        return jnp.where(ge, mid, lo), jnp.where(ge, hi, mid)

    lo, hi = lax.fori_loop(0, TOPK_BISECTIONS, halve, (lo, hi))
    mid = lo * 0.5 + hi * 0.5
    adjacent = (mid == lo) | (mid == hi)

    def surplus(lo, hi):
        need = k - _count(s_ref, ntiles, lambda j, x: x >= hi)
        return need, _count(s_ref, ntiles, lambda j, x: in_bracket(x, lo, hi)) - need

    need, excess = surplus(lo, hi)
    unresolved = jnp.max(jnp.where((excess > 0.5) & ~adjacent, 1.0, 0.0)) > 0.5

    def by_bits():
        def bit_body(it, thr):
            cand = thr + lax.shift_left(jnp.int32(1), 31 - it)
            cnt = _count(s_ref, ntiles, lambda j, x: _sortable(x) >= cand)
            return jnp.where(cnt >= k, cand, thr)

        kth = _unsortable(lax.fori_loop(0, 32, bit_body, jnp.full(shape, INT_MIN, I32)))
        above = _row_reduce(s_ref, ntiles, lambda a, j, x: jnp.minimum(a, jnp.where(x > kth, x, inf)), inf, jnp.min)
        return (kth, above) + surplus(kth, above)

    lo, hi, need, excess = lax.cond(unresolved, by_bits, lambda: (lo, hi, need, excess))
    lane = lax.broadcasted_iota(I32, shape, len(shape) - 1)
    nbits = max(1, int(s_ref.shape[0] * LANE - 1).bit_length())

    def tie_fn():
        def pos_body(it, qp):
            cand = qp + lax.shift_left(jnp.int32(1), nbits - 1 - it)
            cnt = _count(s_ref, ntiles, lambda j, x: in_bracket(x, lo, hi) & (j * LANE + lane < cand))
            return jnp.where(cnt <= need - 1.0, cand, qp)

        return lax.fori_loop(0, nbits, pos_body, jnp.zeros(shape, I32))

    has_tie = jnp.max(excess) > 0.5
    qpos = lax.cond(has_tie, tie_fn, lambda: jnp.full(shape, 2 ** nbits - 1, I32))
    return lo, hi, qpos


def _selected(x, j, lo, hi, qpos):
    lane = lax.broadcasted_iota(I32, x.shape, len(x.shape) - 1)
    return (x >= hi) | ((x >= lo) & (x < hi) & (x > -jnp.inf) & (j * LANE + lane <= qpos))


def _dsa_prompt_kernel(q_ref, iq_ref, g_ref, k_ref, v_ref, ki_ref, bias_ref, o_ref,
                       keys_scr, ki2_scr, wb_scr, logit_scr, m_scr, l_scr, acc_scr, *, topk):
    i = pl.program_id(1)
    ntiles = i + 2
    nt_all = keys_scr.shape[0]
    row = lax.broadcasted_iota(I32, (LANE, LANE), 0)
    lane = lax.broadcasted_iota(I32, (LANE, LANE), 1)
    qpos = N_META + LANE * i + row

    def phys_row(j):
        return pl.multiple_of(jnp.where(j == 0, META_TILE, j - 1) * LANE, LANE)

    @pl.when(i == 0)
    def _():
        z = jnp.zeros((LANE, IDX_D), F32)
        for j in range(nt_all):
            phys = META_TILE if j == 0 else j - 1
            kt = ki_ref[phys * LANE:(phys + 1) * LANE, SM_IK:SM_IK + IDX_D].astype(F32)
            both = jnp.concatenate([jnp.concatenate([kt, z], axis=1), jnp.concatenate([z, kt], axis=1)], axis=0)
            ki2_scr[j] = both.astype(BF16)

    w_all = g_ref[:, SM_IW:SM_IW + IDX_H] * (IDX_D ** -0.5 * IDX_H ** -0.5)
    for h in range(IDX_H):
        wb_scr[h] = jnp.broadcast_to(w_all[:, h:h + 1], (LANE, LANE))

    def score_body(j, carry):
        k2 = ki2_scr[j]
        sc = jnp.zeros((LANE, LANE), F32)
        for hp in range(IDX_H // 2):
            d2 = _dot_nt(iq_ref[:, hp * LANE:(hp + 1) * LANE], k2)
            sc = (sc + jnp.maximum(d2[:, :LANE], 0.0) * wb_scr[2 * hp]
                  + jnp.maximum(d2[:, LANE:], 0.0) * wb_scr[2 * hp + 1])
        kpos = N_META + LANE * (j - 1) + lane
        valid = (kpos >= 0) & (kpos <= qpos)
        keys_scr[j] = jnp.where(valid, sc, -jnp.inf)
        return carry

    lax.fori_loop(0, ntiles, score_body, 0)
    lo, hi, tie_pos = _topk_select(keys_scr, ntiles, topk)

    m_scr[...] = jnp.full(m_scr.shape, NEG_BIG, F32)

    def qk_body(j, carry):
        r0 = phys_row(j)
        sel = _selected(keys_scr[j], j, lo, hi, tie_pos)
        dsel = jnp.minimum(i - (j - 1), 2)
        for h in range(A_H):
            sl = slice(h * A_DH, (h + 1) * A_DH)
            s = _dot_nt(q_ref[:, sl], k_ref[pl.ds(r0, LANE), sl]) * (A_DH ** -0.5) + bias_ref[dsel, h]
            s = jnp.where(sel, s, NEG_BIG)
            logit_scr[h, j] = s
            m_scr[h] = jnp.maximum(m_scr[h], s)
        return carry

    lax.fori_loop(0, ntiles, qk_body, 0)
    for h in range(A_H):
        m_scr[h] = jnp.broadcast_to(jnp.max(m_scr[h], axis=-1, keepdims=True), (LANE, LANE))
    l_scr[...] = jnp.zeros_like(l_scr)
    acc_scr[...] = jnp.zeros_like(acc_scr)

    def pv_body(j, carry):
        r0 = phys_row(j)
        for h in range(A_H):
            sl = slice(h * A_DH, (h + 1) * A_DH)
            p = jnp.exp(logit_scr[h, j] - m_scr[h])
            l_scr[h] += p
            acc_scr[h] += _dot(p.astype(BF16), v_ref[pl.ds(r0, LANE), sl])
        return carry

    lax.fori_loop(0, ntiles, pv_body, 0)
    for h in range(A_H):
        l = jnp.sum(l_scr[h], axis=-1, keepdims=True)
        o_ref[:, h * A_DH:(h + 1) * A_DH] = (acc_scr[h] / l).astype(o_ref.dtype)


def _dsa_prompt(slab_b, slab_f, bias_tiles, topk):
    nb = slab_b.shape[0]
    nq = (T_PAD - LANE) // LANE
    wide = A_H * A_DH
    return pl.pallas_call(
        functools.partial(_dsa_prompt_kernel, topk=topk),
        grid=(nb, nq),
        in_specs=[pl.BlockSpec((None, LANE, wide), lambda b, i: (b, i, COL_AQ // wide)),
                  pl.BlockSpec((None, LANE, wide), lambda b, i: (b, i, COL_IQ // wide)),
                  pl.BlockSpec((None, LANE, LANE), lambda b, i: (b, i, COL_SM // LANE)),
                  _resident((None, T_PAD, wide), lambda b, i: (b, 0, COL_AK // wide)),
                  _resident((None, T_PAD, wide), lambda b, i: (b, 0, COL_AV // wide)),
                  _resident((None, T_PAD, LANE), lambda b, i: (b, 0, COL_SM // LANE)),
                  _resident((3, A_H, LANE, LANE), lambda b, i: (0, 0, 0, 0))],
        out_specs=pl.BlockSpec((None, LANE, wide), lambda b, i: (b, i, 0)),
        out_shape=jax.ShapeDtypeStruct((nb, T_PAD - LANE, wide), BF16),
        scratch_shapes=[pltpu.VMEM((nq + 1, LANE, LANE), F32),
                        pltpu.VMEM((nq + 1, 2 * LANE, 2 * IDX_D), BF16),
                        pltpu.VMEM((IDX_H, LANE, LANE), F32),
                        pltpu.VMEM((A_H, nq + 1, LANE, LANE), F32),
                        pltpu.VMEM((A_H, LANE, LANE), F32),
                        pltpu.VMEM((A_H, LANE, LANE), F32),
                        pltpu.VMEM((A_H, LANE, A_DH), F32)],
        compiler_params=_cparams(("arbitrary", "arbitrary")),
        name="dsa_prompt",
    )(slab_b, slab_b, slab_f, slab_b, slab_b, slab_b, bias_tiles)


def _dsa_sample_score_kernel(pt_ref, iq_ref, g_ref, *refs, seq, n_pages):
    page_refs, o_ref, kpad = refs[:n_pages], refs[n_pages], refs[n_pages + 1]

    @pl.when(pl.program_id(0) == 0)
    def _():
        kpad[...] = jnp.zeros_like(kpad)

    row = lax.broadcasted_iota(I32, (seq, LANE), 0)
    lane = lax.broadcasted_iota(I32, (seq, LANE), 1)
    w_all = g_ref[:, SM_IW:SM_IW + IDX_H] * (IDX_D ** -0.5 * IDX_H ** -0.5)
    wb = [jnp.broadcast_to(w_all[:, h:h + 1], (seq, LANE)) for h in range(IDX_H)]
    iqs = jnp.concatenate([iq_ref[:, h * IDX_D:(h + 1) * IDX_D] for h in range(IDX_H)], axis=0).astype(BF16)

    def scores(kt):
        d = _dot_nt(iqs, kt)
        sc = jnp.zeros((seq, LANE), F32)
        for h in range(IDX_H):
            sc = sc + jnp.maximum(d[h * seq:(h + 1) * seq], 0.0) * wb[h]
        return sc

    for p in range(n_pages):
        o_ref[p] = scores(page_refs[p][...].astype(BF16))
    kpad[0:seq, :] = g_ref[:, SM_IK:SM_IK + IDX_D]
    sc = scores(kpad[...].astype(BF16))
    o_ref[n_pages] = jnp.where(lane <= row, sc, -jnp.inf)


def _page_map(p):
    return lambda b, pt: (pt[b, p], 0, 0)


def _dsa_sample_scores(page_table, slab_f, kidx_pool, seq):
    nb, n_pages = page_table.shape
    page_specs = [pl.BlockSpec((None, PAGE, IDX_D), _page_map(p)) for p in range(n_pages)]
    grid_spec = pltpu.PrefetchScalarGridSpec(
        num_scalar_prefetch=1,
        grid=(nb,),
        in_specs=[pl.BlockSpec((seq, IDX_H * IDX_D), lambda b, pt: (b, COL_IQ // 1024)),
                  pl.BlockSpec((seq, LANE), lambda b, pt: (b, COL_SM // LANE))] + page_specs,
        out_specs=pl.BlockSpec((n_pages + 1, seq, LANE), lambda b, pt: (0, b, 0)),
        scratch_shapes=[pltpu.VMEM((LANE, IDX_D), F32)],
    )
    return pl.pallas_call(
        functools.partial(_dsa_sample_score_kernel, seq=seq, n_pages=n_pages),
        grid_spec=grid_spec,
        out_shape=jax.ShapeDtypeStruct((n_pages + 1, nb * seq, LANE), F32),
        compiler_params=_cparams(("arbitrary",)),
        name="dsa_sample_scores",
    )(page_table, slab_f, slab_f, *([kidx_pool] * n_pages))


def _dsa_select_kernel(keys_ref, lo_ref, hi_ref, pos_ref, *, topk):
    lo, hi, tie_pos = _topk_select(keys_ref, keys_ref.shape[0], topk)
    lo_ref[...] = lo
    hi_ref[...] = hi
    pos_ref[...] = tie_pos


def _dsa_select(keys, topk):
    nt, rows, _ = keys.shape
    row_spec = pl.BlockSpec((LANE, LANE), lambda r: (r, 0))
    return pl.pallas_call(
        functools.partial(_dsa_select_kernel, topk=topk),
        grid=(rows // LANE,),
        in_specs=[pl.BlockSpec((nt, LANE, LANE), lambda r: (0, r, 0))],
        out_specs=[row_spec, row_spec, row_spec],
        out_shape=[jax.ShapeDtypeStruct((rows, LANE), F32),
                   jax.ShapeDtypeStruct((rows, LANE), F32),
                   jax.ShapeDtypeStruct((rows, LANE), I32)],
        compiler_params=_cparams(("arbitrary",)),
        name="dsa_select",
    )(keys)


SA_PG = 4


def _dsa_sample_attend_kernel(pt_ref, q_ref, nk_ref, nv_ref, *refs, seq, n_pages):
    kp, vp = refs[:SA_PG], refs[SA_PG:2 * SA_PG]
    keys_ref, lo_ref, hi_ref, pos_ref, bias_ref, o_ref, s_scr, vb, kpad, vpad = refs[2 * SA_PG:]
    b, g = pl.program_id(0), pl.program_id(1)
    ntile = n_pages + 1

    @pl.when((b == 0) & (g == 0))
    def _():
        kpad[...] = jnp.zeros_like(kpad)
        vpad[...] = jnp.zeros_like(vpad)

    qh = [q_ref[:, h * A_DH:(h + 1) * A_DH].astype(BF16) for h in range(A_H)]
    for j in range(SA_PG):
        t = g * SA_PG + j
        r0 = pl.multiple_of(t * PAGE, PAGE)
        for h in range(A_H):
            kh = kp[j][pl.ds(h, PAGE, stride=A_H), :].astype(BF16)
            s_scr[h, t] = _dot_nt(qh[h], kh)
            vb[h, pl.ds(r0, PAGE), :] = vp[j][pl.ds(h, PAGE, stride=A_H), :].astype(BF16)

    @pl.when(g == pl.num_programs(1) - 1)
    def _():
        kpad[0:seq, :] = nk_ref[...]
        vpad[0:seq, :] = nv_ref[...]
        lo, hi, tie_pos = lo_ref[...], hi_ref[...], pos_ref[...]
        sel = jnp.concatenate([_selected(keys_ref[t], t, lo, hi, tie_pos) for t in range(ntile)], axis=1)
        for h in range(A_H):
            sl = slice(h * A_DH, (h + 1) * A_DH)
            s_scr[h, n_pages] = _dot_nt(qh[h], kpad[:, sl].astype(BF16))
            vb[h, n_pages * PAGE:ntile * PAGE, :] = vpad[:, sl].astype(BF16)
            s = jnp.concatenate([s_scr[h, t] for t in range(ntile)], axis=1) * (A_DH ** -0.5) + bias_ref[h]
            s = jnp.where(sel, s, NEG_BIG)
            p = jnp.where(sel, jnp.exp(s - jnp.max(s, axis=-1, keepdims=True)), 0.0)
            out = _dot(p.astype(BF16), vb[h]) / jnp.sum(p, axis=-1, keepdims=True)
            o_ref[:, sl] = out.astype(o_ref.dtype)


def _group_page_map(j):
    return lambda b, g, pt: (pt[b, g * SA_PG + j], 0, 0)


def _dsa_sample_attend(page_table, slab_f, k_pool, v_pool, keys, lo, hi, tie_pos, bias_s, seq):
    nb, n_pages = page_table.shape
    assert n_pages % SA_PG == 0
    wide = A_H * A_DH
    ntile = n_pages + 1
    page_specs = [pl.BlockSpec((None, PAGE * A_H, A_DH), _group_page_map(j)) for j in range(SA_PG)]
    grid_spec = pltpu.PrefetchScalarGridSpec(
        num_scalar_prefetch=1,
        grid=(nb, n_pages // SA_PG),
        in_specs=[pl.BlockSpec((seq, wide), lambda b, g, pt: (b, COL_AQ // wide)),
                  pl.BlockSpec((seq, wide), lambda b, g, pt: (b, COL_AK // wide)),
                  pl.BlockSpec((seq, wide), lambda b, g, pt: (b, COL_AV // wide))]
                 + page_specs + page_specs
                 + [pl.BlockSpec((ntile, seq, LANE), lambda b, g, pt: (0, b, 0)),
                    pl.BlockSpec((seq, LANE), lambda b, g, pt: (b, 0)),
                    pl.BlockSpec((seq, LANE), lambda b, g, pt: (b, 0)),
                    pl.BlockSpec((seq, LANE), lambda b, g, pt: (b, 0)),
                    _resident((A_H, seq, ntile * LANE), lambda b, g, pt: (0, 0, 0))],
        out_specs=pl.BlockSpec((seq, wide), lambda b, g, pt: (b, 0)),
        scratch_shapes=[pltpu.VMEM((A_H, ntile, seq, LANE), F32),
                        pltpu.VMEM((A_H, ntile * PAGE, A_DH), BF16),
                        pltpu.VMEM((PAGE, wide), F32),
                        pltpu.VMEM((PAGE, wide), F32)],
    )
    return pl.pallas_call(
        functools.partial(_dsa_sample_attend_kernel, seq=seq, n_pages=n_pages),
        grid_spec=grid_spec,
        out_shape=jax.ShapeDtypeStruct((nb * seq, wide), F32),
        compiler_params=_cparams(("arbitrary", "arbitrary")),
        name="dsa_sample_attend",
    )(page_table, slab_f, slab_f, slab_f, *([k_pool] * SA_PG), *([v_pool] * SA_PG), keys, lo, hi, tie_pos, bias_s)


MO_TM = 256


def _mixout_kernel(mm_ref, ma_ref, x_ref, wo_ref, g_ref, wq_ref, k1_ref, k2_ref,
                   hp_ref, xnt_ref, s1_ref, s2_ref):
    half = M_H * M_DV
    y = (_dot(mm_ref[...].astype(BF16), wo_ref[0:half, :])
         + _dot(ma_ref[...].astype(BF16), wo_ref[half:, :]))
    hp = x_ref[...] + y
    hp_ref[...] = hp
    xn = hp * lax.rsqrt(jnp.mean(hp * hp, axis=-1, keepdims=True) + RMS_EPS) * g_ref[...]
    xnt = xn.T.astype(BF16)
    xnt_ref[...] = xnt
    qqt = _dot(wq_ref[...], xnt)
    hq = PEER_DQ // 2
    for h in range(PEER_H):
        q1 = qqt[h * PEER_DQ:h * PEER_DQ + hq, :].astype(BF16)
        q2 = qqt[h * PEER_DQ + hq:(h + 1) * PEER_DQ, :].astype(BF16)
        s1_ref[h] = _dot(k1_ref[h], q1)
        s2_ref[h] = _dot(k2_ref[h], q2)


def _mixout(mm, ma, x2d, wo, g2, wqt, k1, k2):
    rows = x2d.shape[0]
    half = M_H * M_DV
    c2 = lambda i: (0, 0)
    c3 = lambda i: (0, 0, 0)
    return pl.pallas_call(
        _mixout_kernel,
        grid=(rows // MO_TM,),
        in_specs=[pl.BlockSpec((MO_TM, half), lambda i: (i, 0)),
                  pl.BlockSpec((MO_TM, A_H * A_DH), lambda i: (i, 0)),
                  pl.BlockSpec((MO_TM, D_MODEL), lambda i: (i, 0)),
                  _resident((half + A_H * A_DH, D_MODEL), c2),
                  _resident((1, D_MODEL), c2),
                  _resident((PEER_H * PEER_DQ, D_MODEL), c2),
                  _resident((PEER_H, N_KEYS, PEER_DQ // 2), c3),
                  _resident((PEER_H, N_KEYS, PEER_DQ // 2), c3)],
        out_specs=[pl.BlockSpec((MO_TM, D_MODEL), lambda i: (i, 0)),
                   pl.BlockSpec((D_MODEL, MO_TM), lambda i: (0, i)),
                   pl.BlockSpec((PEER_H, N_KEYS, MO_TM), lambda i: (0, 0, i)),
                   pl.BlockSpec((PEER_H, N_KEYS, MO_TM), lambda i: (0, 0, i))],
        out_shape=[jax.ShapeDtypeStruct((rows, D_MODEL), F32),
                   jax.ShapeDtypeStruct((D_MODEL, rows), BF16),
                   jax.ShapeDtypeStruct((PEER_H, N_KEYS, rows), F32),
                   jax.ShapeDtypeStruct((PEER_H, N_KEYS, rows), F32)],
        compiler_params=_cparams(("arbitrary",)),
        name="mixout",
    )(mm, ma, x2d, wo, g2, wqt, k1, k2)


PS_TN = 256


def _extract_top(s, count):
    n = s.shape[0]
    idx = lax.broadcasted_iota(I32, s.shape, 0)
    rank = jnp.full(s.shape, float(count), F32)
    rem = s
    vals = []
    for r in range(count):
        mx = jnp.max(rem, axis=0, keepdims=True)
        first = jnp.min(jnp.where(rem == mx, idx, n), axis=0, keepdims=True)
        hit = idx == first
        rank = jnp.where(hit, float(r), rank)
        rem = jnp.where(hit, -jnp.inf, rem)
        vals.append(mx)
    return vals, rank


def _extract_top_distinct(s, count):
    rank = jnp.full(s.shape, float(count), F32)
    rem = s
    vals = []
    for r in range(count):
        mx = jnp.max(rem, axis=0, keepdims=True)
        hit = rem == mx
        rank = jnp.where(hit, float(r), rank)
        rem = jnp.where(hit, -jnp.inf, rem)
        vals.append(mx)
    marked = jnp.sum(jnp.where(rank < float(count), 1.0, 0.0), axis=0, keepdims=True)
    return vals, rank, marked == float(count)


def _extract_top_auto(s, count):
    vals, rank, ok = _extract_top_distinct(s, count)
    any_tie = jnp.min(jnp.where(ok, 1.0, 0.0)) < 0.5
    return lax.cond(any_tie, lambda: tuple(_extract_top(s, count)), lambda: (vals, rank))


_PAIR_COUNT = [PEER_TOPK // (r1 + 1) for r1 in range(PEER_TOPK)]
_PAIR_START = [sum(_PAIR_COUNT[:r1]) for r1 in range(PEER_TOPK)]
_PAIR_ROWS = -(-sum(_PAIR_COUNT) // 8) * 8


def _peer_select_kernel(s1_ref, s2_ref, li_ref, r2_ref, c1_ref, e2_ref):
    s1 = s1_ref[...]
    s2 = s2_ref[...]
    tn = s1.shape[1]
    v1, rank1 = _extract_top_auto(s1, PEER_TOPK)
    v2, rank2 = _extract_top_auto(s2, PEER_TOPK)
    k = lax.broadcasted_iota(I32, (_PAIR_ROWS, tn), 0)
    in_r1 = [(k >= _PAIR_START[r]) & (k < _PAIR_START[r] + _PAIR_COUNT[r]) for r in range(PEER_TOPK)]
    a1 = jnp.full((_PAIR_ROWS, tn), -jnp.inf, F32)
    start = jnp.zeros((_PAIR_ROWS, tn), I32)
    for r in range(PEER_TOPK):
        a1 = jnp.where(in_r1[r], v1[r], a1)
        start = jnp.where(in_r1[r], _PAIR_START[r], start)
    r2_of_row = k - start
    a2 = jnp.zeros((_PAIR_ROWS, tn), F32)
    for r in range(PEER_TOPK):
        a2 = jnp.where(r2_of_row == r, v2[r], a2)
    sc, crank = _extract_top(a1 + a2, PEER_TOPK)
    chosen = jnp.where(crank < float(PEER_TOPK), 1.0, 0.0)
    li = jnp.zeros(s1.shape, F32)
    for r in range(PEER_TOPK):
        cnt = jnp.sum(jnp.where(in_r1[r], chosen, 0.0), axis=0, keepdims=True)
        li = jnp.where(rank1 == float(r), cnt, li)
    z = jnp.zeros_like(sc[0])
    for c in range(PEER_TOPK):
        z = z + jnp.exp(sc[c] - sc[0])
    li_ref[...] = li
    c1_ref[...] = jnp.exp(s1 - v1[0]) / z
    e2 = jnp.exp(s2 - v2[0])
    for c in range(tn // LANE):
        r2_ref[c] = rank2[:, c * LANE:(c + 1) * LANE]
        e2_ref[c] = e2[:, c * LANE:(c + 1) * LANE]


def _peer_select(s1t, s2t):
    nh, nk, rows = s1t.shape
    spec = pl.BlockSpec((None, nk, PS_TN), lambda t, h: (h, 0, t))
    shp = jax.ShapeDtypeStruct((nh, nk, rows), F32)
    spec2 = pl.BlockSpec((None, PS_TN // LANE, nk, LANE), lambda t, h: (h, t, 0, 0))
    shp2 = jax.ShapeDtypeStruct((nh, rows // LANE, nk, LANE), F32)
    return pl.pallas_call(
        _peer_select_kernel,
        grid=(rows // PS_TN, nh),
        in_specs=[spec, spec],
        out_specs=[spec, spec2, spec, spec2],
        out_shape=[shp, shp2, shp, shp2],
        compiler_params=_cparams(("arbitrary", "arbitrary")),
        name="peer_select",
    )(s1t, s2t)


PD_TN = 512
PD_NI = 8
PD_EB = PD_NI * N_KEYS


def _peer_dense_kernel(xnt_ref, u_ref, v_ref, li_ref, r2_ref, c1_ref, e2_ref, hp_ref, g_ref, o_ref,
                       a0, a1, p0, p1):
    s = pl.program_id(1)
    n_blocks = pl.num_programs(1) - 2

    @pl.when(s == 0)
    def _():
        o_ref[...] = jnp.zeros_like(o_ref)
        for ref in (a0, a1, p0, p1):
            ref[...] = jnp.zeros_like(ref)

    live = (s >= 1) & (s <= n_blocks)

    ntc = PD_TN // LANE

    def step(a_cur, a_prev, p_cur, p_prev):
        pt = jnp.concatenate([p_cur[tc] for tc in range(ntc)], axis=1)
        ocw = D_MODEL // PD_NI
        for il in range(PD_NI):
            rows = slice(il * N_KEYS, (il + 1) * N_KEYS)
            ocols = slice(il * ocw, (il + 1) * ocw)
            o_ref[:, ocols] += _dot_tn(pt, v_ref[:, ocols])
            pre = _dot(u_ref[rows, :], xnt_ref[...])
            for tc in range(ntc):
                a_cur[tc, rows, :] = pre[:, tc * LANE:(tc + 1) * LANE]
            ng = PD_NI // (PD_NI // ntc)
            grp, tc = divmod(il, ntc)
            cols = slice(tc * LANE, (tc + 1) * LANE)
            hj = N_KEYS // 2
            for jh in range(2):
                jrows = slice(jh * hj, (jh + 1) * hj)
                w = [jnp.zeros((hj, LANE), F32) for _ in range(ng)]
                for h in range(PEER_H):
                    r2 = r2_ref[h, tc, jrows, :]
                    e2 = e2_ref[h, tc, jrows, :]
                    for q in range(ng):
                        i1 = grp * ng + q
                        li = li_ref[h, i1:i1 + 1, cols]
                        c1 = c1_ref[h, i1:i1 + 1, cols]
                        w[q] = w[q] + jnp.where(r2 < li, e2, 0.0) * c1
                for q in range(ng):
                    erows = slice((grp * ng + q) * N_KEYS + jh * hj, (grp * ng + q) * N_KEYS + (jh + 1) * hj)
                    a = a_prev[tc, erows, :]
                    act = 0.5 * a * (1.0 + lax.erf(a * (2.0 ** -0.5)))
                    p_prev[tc, erows, :] = jnp.where(live, w[q] * act, 0.0).astype(BF16)

    parity = lax.rem(s, 2)
    pl.when(parity == 0)(lambda: step(a0, a1, p0, p1))
    pl.when(parity == 1)(lambda: step(a1, a0, p1, p0))

    @pl.when(s == pl.num_programs(1) - 1)
    def _():
        h = hp_ref[...] + o_ref[...]
        o_ref[...] = h * lax.rsqrt(jnp.mean(h * h, axis=-1, keepdims=True) + RMS_EPS) * g_ref[...]


def _peer_dense(xnt, u, v, li, r2, c1, e2, hp, gf):
    rows = xnt.shape[1]
    n_blocks = N_EXPERTS // PD_EB

    def blk(s, lag):
        return jnp.clip(s - lag, 0, n_blocks - 1)

    key2_spec = pl.BlockSpec((PEER_H, PD_TN // LANE, N_KEYS, LANE), lambda t, s: (0, t, 0, 0))
    key1_spec = pl.BlockSpec((PEER_H, PD_NI, PD_TN), lambda t, s: (0, blk(s, 1), t))
    return pl.pallas_call(
        _peer_dense_kernel,
        grid=(rows // PD_TN, n_blocks + 2),
        in_specs=[pl.BlockSpec((D_MODEL, PD_TN), lambda t, s: (0, t)),
                  pl.BlockSpec((PD_EB, D_MODEL), lambda t, s: (blk(s, 0), 0)),
                  pl.BlockSpec((PD_EB, D_MODEL), lambda t, s: (blk(s, 2), 0)),
                  key1_spec, key2_spec, key1_spec, key2_spec,
                  pl.BlockSpec((PD_TN, D_MODEL), lambda t, s: (t, 0)),
                  pl.BlockSpec((1, D_MODEL), lambda t, s: (0, 0))],
        out_specs=pl.BlockSpec((PD_TN, D_MODEL), lambda t, s: (t, 0)),
        out_shape=jax.ShapeDtypeStruct((rows, D_MODEL), F32),
        scratch_shapes=[pltpu.VMEM((PD_TN // LANE, PD_EB, LANE), F32), pltpu.VMEM((PD_TN // LANE, PD_EB, LANE), F32),
                        pltpu.VMEM((PD_TN // LANE, PD_EB, LANE), BF16), pltpu.VMEM((PD_TN // LANE, PD_EB, LANE), BF16)],
        compiler_params=_cparams(("arbitrary", "arbitrary")),
        name="peer_dense",
    )(xnt, u, v, li, r2, c1, e2, hp, gf)


def _t5_bucket_table(n):
    rel = np.arange(n)
    max_exact = REL_BUCKETS // 2
    nf = np.maximum(rel, 1).astype(np.float32)
    large = max_exact + (np.log(nf / np.float32(max_exact)) / np.float32(math.log(REL_MAX_DIST / max_exact))
                         * np.float32(REL_BUCKETS - max_exact)).astype(np.int32)
    large = np.minimum(large, REL_BUCKETS - 1)
    return np.where(rel < max_exact, rel, large).astype(np.int32)


def _bias_tiles(rel_bias):
    table = _t5_bucket_table(3 * LANE)
    r = np.arange(LANE)[:, None]
    c = np.arange(LANE)[None, :]
    rel = np.stack([np.maximum(r - c, 0), LANE + r - c, np.full((LANE, LANE), 2 * LANE)])
    assert table[LANE + 1] == REL_BUCKETS - 1
    buckets = table[rel]
    return _bucket_lookup(rel_bias, buckets[:, None])


def _bucket_lookup(rel_bias, buckets):
    tb = rel_bias.astype(F32)
    b = jnp.asarray(buckets)
    hshape = (1, A_H) + (1,) * (buckets.ndim - 2)
    out = jnp.zeros(buckets.shape[:1] + (A_H,) + buckets.shape[2:], F32)
    for k in range(REL_BUCKETS):
        out = jnp.where(b == k, tb[k].reshape(hshape), out)
    return out


def _sample_bias(rel_bias, past, seq, ntile):
    t = np.arange(seq)[:, None]
    c = np.arange(ntile * LANE)[None, :]
    buckets = _t5_bucket_table(past + seq)[np.maximum(past + t - c, 0)]
    return _bucket_lookup(rel_bias, buckets[None, None])[0]


def _relayout_w_in(w):
    cuts = np.cumsum(IN_SPLITS)[:-1].tolist()
    mq, mk, mv, mo, ig, fg, aq, ak, av, iq, ik, iw = jnp.split(w, cuts, axis=-1)
    pad = jnp.zeros((w.shape[0], P_PAD - COL_SM - IDX_D - IDX_H - 2 * M_H), w.dtype)
    return jnp.concatenate([aq, ak, av, iq, mv, mo, mq, mk, ik, iw, ig, fg, pad], axis=-1).astype(BF16)


def kernel(x_prompt, x_sample, cache_k, cache_v, cache_kidx, state_C, state_n, state_m, page_table,
           meta_tokens, rel_bias, final_norm_g, norm1_g, w_in, b_i, b_f, mh_norm_g, w_out, norm2_g,
           peer_wq, peer_k1, peer_k2, peer_u, peer_v):
    B, S_p, D = x_prompt.shape
    Bs, Ts, _ = x_sample.shape
    n_pages = page_table.shape[1]
    depth = w_in.shape[0]
    assert depth == 1 and D == D_MODEL and N_META + S_p == T_PAD - META_OFF and S_p % LANE == 0
    assert n_pages * PAGE + Ts <= (n_pages + 1) * PAGE and cache_k.shape[2] == PAGE
    topk_p = min(TOPK_MAX, S_p // 4)
    topk_s = min(TOPK_MAX, (n_pages * PAGE + Ts) // 4)
    l = 0

    w1 = _relayout_w_in(w_in[l])
    g1 = norm1_g[l].astype(F32).reshape(1, D)
    g2 = norm2_g[l].astype(F32).reshape(1, D)
    gf = final_norm_g.astype(F32).reshape(1, D)
    zpad = jnp.zeros((LANE - SM_FG - M_H,), F32)
    bias_row = jnp.concatenate([jnp.zeros((SM_IG,), F32), b_i[l].astype(F32), b_f[l].astype(F32), zpad]).reshape(1, LANE)
    mhg = mh_norm_g[l].astype(F32).reshape(1, M_H * M_DV)
    wo = w_out[l].astype(BF16)
    wqt = peer_wq[l].T.astype(BF16)
    k1 = peer_k1[l].astype(BF16)
    k2 = peer_k2[l].astype(BF16)
    u = peer_u[l].astype(BF16)
    v = peer_v[l].astype(BF16)
    bias_tiles = _bias_tiles(rel_bias)

    meta = jnp.broadcast_to(meta_tokens.astype(F32)[None], (B, N_META, D))
    xp_all = jnp.concatenate([x_prompt, jnp.zeros((B, META_OFF, D), F32), meta], axis=1)

    pf, pb = _inproj(xp_all.reshape(B * T_PAD, D), g1, w1)
    pf = pf.reshape(B, T_PAD, P_PAD)
    pb = pb.reshape(B, T_PAD, P_PAD)
    mm_p, c_p, n_p, m_p = _mlstm_prompt(pb, pf, bias_row, mhg)
    ma_p = _dsa_prompt(pb, pf, bias_tiles, topk_p)
    xp2 = x_prompt.reshape(B * S_p, D)
    hp_p, xnt_p, s1_p, s2_p = _mixout(mm_p.reshape(B * S_p, -1), ma_p.reshape(B * S_p, -1), xp2, wo, g2, wqt, k1, k2)
    y_prompt = _peer_dense(xnt_p, u, v, *_peer_select(s1_p, s2_p), hp_p, gf).reshape(B, S_p, D)

    xs2 = x_sample.reshape(Bs * Ts, D)
    sf, _ = _inproj(xs2, g1, w1)
    c0 = state_C[l].astype(F32)
    n0 = state_n[l].astype(F32).reshape(Bs, M_H, 1, M_DK)
    m0 = jnp.broadcast_to(state_m[l].astype(F32)[:, :, None, None], (Bs, M_H, 1, LANE))
    mm_s, c_s, n_s, m_s = _mlstm_sample(sf, bias_row, mhg, c0, n0, m0, Ts)
    n_pool = cache_k.shape[1]
    keys_s = _dsa_sample_scores(page_table, sf, cache_kidx[l].reshape(n_pool, PAGE, IDX_D), Ts)
    lo_s, hi_s, pos_s = _dsa_select(keys_s, topk_s)
    bias_s = _sample_bias(rel_bias, n_pages * PAGE, Ts, n_pages + 1)
    ma_s = _dsa_sample_attend(page_table, sf, cache_k[l].reshape(n_pool, PAGE * A_H, A_DH),
                              cache_v[l].reshape(n_pool, PAGE * A_H, A_DH), keys_s, lo_s, hi_s, pos_s, bias_s, Ts)
    hp_s, xnt_s, s1_s, s2_s = _mixout(mm_s, ma_s, xs2, wo, g2, wqt, k1, k2)
    y_sample = _peer_dense(xnt_s, u, v, *_peer_select(s1_s, s2_s), hp_s, gf).reshape(Bs, Ts, D)

    def seq_rows(a):
        return jnp.concatenate([a[:, T_PAD - N_META:], a[:, :S_p]], axis=1)

    wide = A_H * A_DH
    T = N_META + S_p
    k_prompt = seq_rows(pf[:, :, COL_AK:COL_AK + wide]).reshape(1, B, T, A_H, A_DH)
    v_prompt = seq_rows(pf[:, :, COL_AV:COL_AV + wide]).reshape(1, B, T, A_H, A_DH)
    kidx_prompt = seq_rows(pf[:, :, COL_SM + SM_IK:COL_SM + SM_IK + IDX_D]).reshape(1, B, T, IDX_D)
    k_sample = sf[:, COL_AK:COL_AK + wide].reshape(1, Bs, Ts, A_H, A_DH)
    v_sample = sf[:, COL_AV:COL_AV + wide].reshape(1, Bs, Ts, A_H, A_DH)
    kidx_sample = sf[:, COL_SM + SM_IK:COL_SM + SM_IK + IDX_D].reshape(1, Bs, Ts, IDX_D)
    return (y_prompt, y_sample,
            k_prompt, v_prompt, kidx_prompt,
            c_p[None], n_p[:, :, 0, :][None], m_p[:, :, 0, 0][None],
            k_sample, v_sample, kidx_sample,
            c_s[None], n_s[:, :, 0, :][None], m_s[:, :, 0, 0][None])
```

```python
import functools
import math

import numpy as np
import jax
import jax.numpy as jnp
from jax import lax
from jax.experimental import pallas as pl
from jax.experimental.pallas import tpu as pltpu

F32 = jnp.float32
BF16 = jnp.bfloat16
I32 = jnp.int32

D_MODEL = 2048
N_META = 16
RMS_EPS = 1e-6
M_H, M_DK, M_DV = 4, 128, 256
A_H, A_DH = 8, 128
IDX_H, IDX_D = 16, 64
TOPK_MAX = 256
REL_BUCKETS, REL_MAX_DIST = 32, 128
PEER_H, PEER_DQ, N_KEYS, PEER_TOPK = 8, 256, 128, 16
N_EXPERTS = N_KEYS * N_KEYS
PAGE = 128
IN_SPLITS = (M_H * M_DK, M_H * M_DK, M_H * M_DV, M_H * M_DV, M_H, M_H,
             A_H * A_DH, A_H * A_DH, A_H * A_DH, IDX_H * IDX_D, IDX_D, IDX_H)

LANE = 128
VMEM_LIMIT = 56 * 1024 * 1024

COL_AQ, COL_AK, COL_AV, COL_IQ, COL_MV, COL_MO, COL_MQ, COL_MK, COL_SM = (
    0, 1024, 2048, 3072, 4096, 5120, 6144, 6656, 7168)
P_PAD = COL_SM + 512
SM_IK, SM_IW, SM_IG, SM_FG = 0, 64, 80, 84
T_PAD = 17 * LANE
META_TILE = 16
META_OFF = LANE - N_META
INT_MIN = -(2 ** 31)
NEG_BIG = -1e30


def _cparams(sem, vmem=VMEM_LIMIT):
    return pltpu.CompilerParams(dimension_semantics=sem, vmem_limit_bytes=vmem)


def _resident(shape, index_map):
    return pl.BlockSpec(shape, index_map, pipeline_mode=pl.Buffered(1))


def _dot(a, b):
    return jnp.dot(a, b, preferred_element_type=F32)


def _dot_nt(a, b):
    return lax.dot_general(a, b, (((1,), (1,)), ((), ())), preferred_element_type=F32)


def _dot_tn(a, b):
    return lax.dot_general(a, b, (((0,), (0,)), ((), ())), preferred_element_type=F32)


IP_TM_MAX, IP_TN = 1088, 768


def _inproj_kernel(x_ref, g_ref, w_ref, of_ref, ob_ref, hn_ref):
    @pl.when(pl.program_id(1) == 0)
    def _():
        x = x_ref[...]
        y = x * lax.rsqrt(jnp.mean(x * x, axis=-1, keepdims=True) + RMS_EPS) * g_ref[...]
        hn_ref[...] = y.astype(BF16)

    acc = _dot(hn_ref[...], w_ref[...])
    of_ref[...] = acc
    ob_ref[...] = acc.astype(BF16)


def _inproj(x2d, g, w):
    rows = x2d.shape[0]
    tm = IP_TM_MAX if rows % IP_TM_MAX == 0 else min(rows, 1024)
    assert rows % tm == 0
    grid = (rows // tm, P_PAD // IP_TN)
    return pl.pallas_call(
        _inproj_kernel,
        grid=grid,
        in_specs=[pl.BlockSpec((tm, D_MODEL), lambda i, j: (i, 0)),
                  pl.BlockSpec((1, D_MODEL), lambda i, j: (0, 0)),
                  pl.BlockSpec((D_MODEL, IP_TN), lambda i, j: (0, j))],
        out_specs=[pl.BlockSpec((tm, IP_TN), lambda i, j: (i, j)),
                   pl.BlockSpec((tm, IP_TN), lambda i, j: (i, j))],
        out_shape=[jax.ShapeDtypeStruct((rows, P_PAD), F32),
                   jax.ShapeDtypeStruct((rows, P_PAD), BF16)],
        scratch_shapes=[pltpu.VMEM((tm, D_MODEL), BF16)],
        compiler_params=_cparams(("arbitrary", "arbitrary")),
        name="inproj",
    )(x2d, g, w)


def _mlstm_gates(gates, bias_row, lo, hi):
    row = lax.broadcasted_iota(I32, (LANE, LANE), 0)
    col = lax.broadcasted_iota(I32, (LANE, LANE), 1)
    valid = (row >= lo) & (row < hi)
    pre = gates + bias_row
    li = jnp.where(valid, pre, -jnp.inf)
    lf = jnp.where(valid, jax.nn.log_sigmoid(pre), 0.0)
    tril = jnp.where(col <= row, 1.0, 0.0).astype(F32)
    b = jnp.dot(tril, lf, preferred_element_type=F32, precision=lax.Precision.HIGHEST)
    return li, b, li.T, b.T


def _mlstm_head(q, k, v, li, b, lit, bt, h, c_old, n_old, m_old):
    li_col = li[:, SM_IG + h:SM_IG + h + 1]
    b_col = b[:, SM_FG + h:SM_FG + h + 1]
    li_row = lit[SM_IG + h:SM_IG + h + 1, :]
    b_row = bt[SM_FG + h:SM_FG + h + 1, :]
    row = lax.broadcasted_iota(I32, (LANE, LANE), 0)
    col = lax.broadcasted_iota(I32, (LANE, LANE), 1)
    dmat = jnp.where(col <= row, b_col - b_row + li_row, -jnp.inf)
    inter = b_col + m_old
    m_t = jnp.maximum(inter, jnp.max(dmat, axis=-1, keepdims=True))
    w_inter = jnp.exp(inter - m_t)
    scale = M_DK ** -0.5
    s = _dot_nt(q, k) * scale * jnp.exp(dmat - m_t)
    num = w_inter * _dot(q, c_old.astype(BF16)) + _dot(s.astype(BF16), v)
    qf = q.astype(F32)
    den = w_inter * jnp.sum(qf * n_old, axis=-1, keepdims=True) + jnp.sum(s, axis=-1, keepdims=True)
    h_out = num / jnp.maximum(jnp.abs(den), jnp.exp(-m_t))
    b_last = b_col[LANE - 1:LANE, :]
    g_col = b_last - b_col + li_col
    m_new = jnp.maximum(b_last + m_old, jnp.max(g_col, axis=0, keepdims=True))
    a = jnp.exp(b_last + m_old - m_new)
    wk = jnp.exp(g_col - m_new)
    wv = (wk * v.astype(F32)).astype(BF16)
    c_new = a * c_old + scale * _dot_tn(k, wv)
    n_new = a * n_old + scale * jnp.sum(wk * k.astype(F32), axis=0, keepdims=True)
    return h_out, c_new, n_new, m_new


def _mlstm_mix(h_out, mo, g):
    hn = h_out * lax.rsqrt(jnp.mean(h_out * h_out, axis=-1, keepdims=True) + RMS_EPS)
    return hn * g * jax.nn.sigmoid(mo)


def _mlstm_prompt_kernel(q_ref, k_ref, v_ref, mo_ref, g_ref, bias_ref, mhg_ref,
                         out_ref, c_out, n_out, m_out, c_scr, n_scr, m_scr):
    c = pl.program_id(1)

    @pl.when(c == 0)
    def _():
        c_scr[...] = jnp.zeros_like(c_scr)
        n_scr[...] = jnp.zeros_like(n_scr)
        m_scr[...] = jnp.zeros_like(m_scr)

    lo = jnp.where(c == 0, META_OFF, 0)
    li, b, lit, bt = _mlstm_gates(g_ref[...], bias_ref[...], lo, LANE)
    for h in range(M_H):
        q = q_ref[:, h * M_DK:(h + 1) * M_DK]
        k = k_ref[:, h * M_DK:(h + 1) * M_DK]
        v = v_ref[:, h * M_DV:(h + 1) * M_DV]
        h_out, c_new, n_new, m_new = _mlstm_head(
            q, k, v, li, b, lit, bt, h, c_scr[h], n_scr[h], m_scr[h][:, 0:1])
        c_scr[h] = c_new
        n_scr[h] = n_new
        m_scr[h] = jnp.broadcast_to(m_new, (1, LANE))
        sl = slice(h * M_DV, (h + 1) * M_DV)
        out_ref[:, sl] = _mlstm_mix(h_out, mo_ref[:, sl], mhg_ref[:, sl]).astype(out_ref.dtype)

    @pl.when(c == pl.num_programs(1) - 1)
    def _():
        c_out[...] = c_scr[...]
        n_out[...] = n_scr[...]
        m_out[...] = m_scr[...]


def _mlstm_prompt(slab_b, slab_f, bias_row, mhg):
    nb = slab_b.shape[0]
    nchunk = T_PAD // LANE

    def tile(c):
        return (c + META_TILE) % nchunk

    return pl.pallas_call(
        _mlstm_prompt_kernel,
        grid=(nb, nchunk),
        in_specs=[pl.BlockSpec((None, LANE, M_H * M_DK), lambda b, c: (b, tile(c), COL_MQ // 512)),
                  pl.BlockSpec((None, LANE, M_H * M_DK), lambda b, c: (b, tile(c), COL_MK // 512)),
                  pl.BlockSpec((None, LANE, M_H * M_DV), lambda b, c: (b, tile(c), COL_MV // 1024)),
                  pl.BlockSpec((None, LANE, M_H * M_DV), lambda b, c: (b, tile(c), COL_MO // 1024)),
                  pl.BlockSpec((None, LANE, LANE), lambda b, c: (b, tile(c), COL_SM // LANE)),
                  pl.BlockSpec((1, LANE), lambda b, c: (0, 0)),
                  pl.BlockSpec((1, M_H * M_DV), lambda b, c: (0, 0))],
        out_specs=[pl.BlockSpec((None, LANE, M_H * M_DV), lambda b, c: (b, jnp.maximum(c - 1, 0), 0)),
                   pl.BlockSpec((None, M_H, M_DK, M_DV), lambda b, c: (b, 0, 0, 0)),
                   pl.BlockSpec((None, M_H, 1, M_DK), lambda b, c: (b, 0, 0, 0)),
                   pl.BlockSpec((None, M_H, 1, LANE), lambda b, c: (b, 0, 0, 0))],
        out_shape=[jax.ShapeDtypeStruct((nb, T_PAD - LANE, M_H * M_DV), BF16),
                   jax.ShapeDtypeStruct((nb, M_H, M_DK, M_DV), F32),
                   jax.ShapeDtypeStruct((nb, M_H, 1, M_DK), F32),
                   jax.ShapeDtypeStruct((nb, M_H, 1, LANE), F32)],
        scratch_shapes=[pltpu.VMEM((M_H, M_DK, M_DV), F32),
                        pltpu.VMEM((M_H, 1, M_DK), F32),
                        pltpu.VMEM((M_H, 1, LANE), F32)],
        compiler_params=_cparams(("arbitrary", "arbitrary")),
        name="mlstm_prompt",
    )(slab_b, slab_b, slab_b, slab_f, slab_f, bias_row, mhg)


MS_NB = 2


def _mlstm_sample_kernel(q_ref, k_ref, v_ref, mo_ref, g_ref, bias_ref, mhg_ref, c_in, n_in, m_in,
                         out_ref, c_out, n_out, m_out, qp, kp, vp, gp, *, seq):
    @pl.when(pl.program_id(0) == 0)
    def _():
        qp[...] = jnp.zeros_like(qp)
        kp[...] = jnp.zeros_like(kp)
        vp[...] = jnp.zeros_like(vp)
        gp[...] = jnp.zeros_like(gp)

    for s in range(MS_NB):
        rows = slice(s * seq, (s + 1) * seq)
        qp[0:seq, :] = q_ref[rows, :]
        kp[0:seq, :] = k_ref[rows, :]
        vp[0:seq, :] = v_ref[rows, :]
        gp[0:seq, :] = g_ref[rows, :]
        li, b, lit, bt = _mlstm_gates(gp[...], bias_ref[...], 0, seq)
        for h in range(M_H):
            q = qp[:, h * M_DK:(h + 1) * M_DK].astype(BF16)
            k = kp[:, h * M_DK:(h + 1) * M_DK].astype(BF16)
            v = vp[:, h * M_DV:(h + 1) * M_DV].astype(BF16)
            h_out, c_new, n_new, m_new = _mlstm_head(
                q, k, v, li, b, lit, bt, h, c_in[s, h], n_in[s, h], m_in[s, h][:, 0:1])
            c_out[s, h] = c_new
            n_out[s, h] = n_new
            m_out[s, h] = jnp.broadcast_to(m_new, (1, LANE))
            sl = slice(h * M_DV, (h + 1) * M_DV)
            mixed = _mlstm_mix(h_out[0:seq], mo_ref[rows, sl], mhg_ref[:, sl])
            out_ref[rows, sl] = mixed.astype(out_ref.dtype)


def _mlstm_sample(slab_f, bias_row, mhg, c0, n0, m0, seq):
    nb = c0.shape[0]
    rb = MS_NB * seq
    st4 = lambda i: (i, 0, 0, 0)
    return pl.pallas_call(
        functools.partial(_mlstm_sample_kernel, seq=seq),
        grid=(nb // MS_NB,),
        in_specs=[pl.BlockSpec((rb, M_H * M_DK), lambda i: (i, COL_MQ // 512)),
                  pl.BlockSpec((rb, M_H * M_DK), lambda i: (i, COL_MK // 512)),
                  pl.BlockSpec((rb, M_H * M_DV), lambda i: (i, COL_MV // 1024)),
                  pl.BlockSpec((rb, M_H * M_DV), lambda i: (i, COL_MO // 1024)),
                  pl.BlockSpec((rb, LANE), lambda i: (i, COL_SM // LANE)),
                  pl.BlockSpec((1, LANE), lambda i: (0, 0)),
                  pl.BlockSpec((1, M_H * M_DV), lambda i: (0, 0)),
                  pl.BlockSpec((MS_NB, M_H, M_DK, M_DV), st4),
                  pl.BlockSpec((MS_NB, M_H, 1, M_DK), st4),
                  pl.BlockSpec((MS_NB, M_H, 1, LANE), st4)],
        out_specs=[pl.BlockSpec((rb, M_H * M_DV), lambda i: (i, 0)),
                   pl.BlockSpec((MS_NB, M_H, M_DK, M_DV), st4),
                   pl.BlockSpec((MS_NB, M_H, 1, M_DK), st4),
                   pl.BlockSpec((MS_NB, M_H, 1, LANE), st4)],
        out_shape=[jax.ShapeDtypeStruct((nb * seq, M_H * M_DV), BF16),
                   jax.ShapeDtypeStruct((nb, M_H, M_DK, M_DV), F32),
                   jax.ShapeDtypeStruct((nb, M_H, 1, M_DK), F32),
                   jax.ShapeDtypeStruct((nb, M_H, 1, LANE), F32)],
        scratch_shapes=[pltpu.VMEM((LANE, M_H * M_DK), F32),
                        pltpu.VMEM((LANE, M_H * M_DK), F32),
                        pltpu.VMEM((LANE, M_H * M_DV), F32),
                        pltpu.VMEM((LANE, LANE), F32)],
        compiler_params=_cparams(("arbitrary",)),
        name="mlstm_sample",
    )(slab_f, slab_f, slab_f, slab_f, slab_f, bias_row, mhg, c0, n0, m0)


def _sortable(x):
    x = jnp.where(x == 0.0, 0.0, x)
    bits = lax.bitcast_convert_type(x, I32)
    return bits ^ ((bits >> 31) & jnp.int32(0x7FFFFFFF))


def _unsortable(key):
    return lax.bitcast_convert_type(key ^ ((key >> 31) & jnp.int32(0x7FFFFFFF)), F32)


def _row_reduce(s_ref, ntiles, fn, init, reduce):
    shape = s_ref.shape[1:]
    acc = lax.fori_loop(0, ntiles, lambda j, a: fn(a, j, s_ref[j]), jnp.full(shape, init, F32))
    return jnp.broadcast_to(reduce(acc, axis=-1, keepdims=True), shape)


def _count(s_ref, ntiles, pred):
    return _row_reduce(s_ref, ntiles, lambda a, j, x: a + jnp.where(pred(j, x), 1.0, 0.0), 0.0, jnp.sum)


TOPK_BISECTIONS = 32


def _topk_select(s_ref, ntiles, topk):
    shape = s_ref.shape[1:]
    inf = jnp.inf
    vmax = _row_reduce(s_ref, ntiles, lambda a, j, x: jnp.maximum(a, x), -inf, jnp.max)
    vmin = _row_reduce(s_ref, ntiles, lambda a, j, x: jnp.minimum(a, jnp.where(x > -inf, x, inf)), inf, jnp.min)
    n_valid = _count(s_ref, ntiles, lambda j, x: x > -inf)
    k = jnp.minimum(float(topk), n_valid)

    def in_bracket(x, lo, hi):
        return (x >= lo) & (x < hi) & (x > -inf)

    top_tied = _count(s_ref, ntiles, lambda j, x: x >= vmax) >= k
    lo = jnp.where(top_tied, vmax, vmin)
    hi = jnp.where(top_tied, inf, vmax)

    def halve(_, c):
        lo, hi = c
        mid = lo * 0.5 + hi * 0.5
        ge = _count(s_ref, ntiles, lambda j, x: x >= mid) >= k
        return jnp.where(ge, mid, lo), jnp.where(ge, hi, mid)

    lo, hi = lax.fori_loop(0, TOPK_BISECTIONS, halve, (lo, hi))
    mid = lo * 0.5 + hi * 0.5
    adjacent = (mid == lo) | (mid == hi)

    def surplus(lo, hi):
        need = k - _count(s_ref, ntiles, lambda j, x: x >= hi)
        return need, _count(s_ref, ntiles, lambda j, x: in_bracket(x, lo, hi)) - need

    need, excess = surplus(lo, hi)
    unresolved = jnp.max(jnp.where((excess > 0.5) & ~adjacent, 1.0, 0.0)) > 0.5

    def by_bits():
        def bit_body(it, thr):
            cand = thr + lax.shift_left(jnp.int32(1), 31 - it)
            cnt = _count(s_ref, ntiles, lambda j, x: _sortable(x) >= cand)
            return jnp.where(cnt >= k, cand, thr)

        kth = _unsortable(lax.fori_loop(0, 32, bit_body, jnp.full(shape, INT_MIN, I32)))
        above = _row_reduce(s_ref, ntiles, lambda a, j, x: jnp.minimum(a, jnp.where(x > kth, x, inf)), inf, jnp.min)
        return (kth, above) + surplus(kth, above)

    lo, hi, need, excess = lax.cond(unresolved, by_bits, lambda: (lo, hi, need, excess))
    lane = lax.broadcasted_iota(I32, shape, len(shape) - 1)
    nbits = max(1, int(s_ref.shape[0] * LANE - 1).bit_length())

    def tie_fn():
        def pos_body(it, qp):
            cand = qp + lax.shift_left(jnp.int32(1), nbits - 1 - it)
            cnt = _count(s_ref, ntiles, lambda j, x: in_bracket(x, lo, hi) & (j * LANE + lane < cand))
            return jnp.where(cnt <= need - 1.0, cand, qp)

        return lax.fori_loop(0, nbits, pos_body, jnp.zeros(shape, I32))

    has_tie = jnp.max(excess) > 0.5
    qpos = lax.cond(has_tie, tie_fn, lambda: jnp.full(shape, 2 ** nbits - 1, I32))
    return lo, hi, qpos


def _selected(x, j, lo, hi, qpos):
    lane = lax.broadcasted_iota(I32, x.shape, len(x.shape) - 1)
    return (x >= hi) | ((x >= lo) & (x < hi) & (x > -jnp.inf) & (j * LANE + lane <= qpos))


DP_QT = 2
DP_TQ = DP_QT * LANE


def _dsa_prompt_kernel(q_ref, iq_ref, g_ref, k_ref, v_ref, ki_ref, bias_ref, o_ref,
                       keys_scr, ki2_scr, wb_scr, logit_scr, m_scr, l_scr, acc_scr, *, topk):
    i = pl.program_id(1)
    ntiles = DP_QT * (i + 1) + 1
    nt_all = keys_scr.shape[0]
    row = lax.broadcasted_iota(I32, (DP_TQ, LANE), 0)
    lane = lax.broadcasted_iota(I32, (DP_TQ, LANE), 1)
    qpos = N_META + DP_TQ * i + row

    def phys_row(j):
        return pl.multiple_of(jnp.where(j == 0, META_TILE, j - 1) * LANE, LANE)

    @pl.when(i == 0)
    def _():
        z = jnp.zeros((LANE, IDX_D), F32)
        for j in range(nt_all):
            phys = META_TILE if j == 0 else j - 1
            kt = ki_ref[phys * LANE:(phys + 1) * LANE, SM_IK:SM_IK + IDX_D].astype(F32)
            both = jnp.concatenate([jnp.concatenate([kt, z], axis=1), jnp.concatenate([z, kt], axis=1)], axis=0)
            ki2_scr[j] = both.astype(BF16)

    w_all = g_ref[:, SM_IW:SM_IW + IDX_H] * (IDX_D ** -0.5 * IDX_H ** -0.5)
    for h in range(IDX_H):
        wb_scr[h] = jnp.broadcast_to(w_all[:, h:h + 1], (DP_TQ, LANE))

    def score_body(j, carry):
        k2 = ki2_scr[j]
        sc = jnp.zeros((DP_TQ, LANE), F32)
        for hp in range(IDX_H // 2):
            d2 = _dot_nt(iq_ref[:, hp * LANE:(hp + 1) * LANE], k2)
            sc = (sc + jnp.maximum(d2[:, :LANE], 0.0) * wb_scr[2 * hp]
                  + jnp.maximum(d2[:, LANE:], 0.0) * wb_scr[2 * hp + 1])
        kpos = N_META + LANE * (j - 1) + lane
        valid = (kpos >= 0) & (kpos <= qpos)
        keys_scr[j] = jnp.where(valid, sc, -jnp.inf)
        return carry

    lax.fori_loop(0, ntiles, score_body, 0)
    lo, hi, tie_pos = _topk_select(keys_scr, ntiles, topk)

    m_scr[...] = jnp.full(m_scr.shape, NEG_BIG, F32)

    def qk_body(j, carry):
        r0 = phys_row(j)
        sel = _selected(keys_scr[j], j, lo, hi, tie_pos)
        dsel = [jnp.clip(DP_QT * i + sub - (j - 1), 0, 2) for sub in range(DP_QT)]
        for h in range(A_H):
            sl = slice(h * A_DH, (h + 1) * A_DH)
            bias = jnp.concatenate([bias_ref[d, h] for d in dsel], axis=0)
            s = _dot_nt(q_ref[:, sl], k_ref[pl.ds(r0, LANE), sl]) * (A_DH ** -0.5) + bias
            s = jnp.where(sel, s, NEG_BIG)
            logit_scr[h, j] = s
            m_scr[h] = jnp.maximum(m_scr[h], s)
        return carry

    lax.fori_loop(0, ntiles, qk_body, 0)
    for h in range(A_H):
        m_scr[h] = jnp.broadcast_to(jnp.max(m_scr[h], axis=-1, keepdims=True), (DP_TQ, LANE))
    l_scr[...] = jnp.zeros_like(l_scr)
    acc_scr[...] = jnp.zeros_like(acc_scr)

    def pv_body(j, carry):
        r0 = phys_row(j)
        for h in range(A_H):
            sl = slice(h * A_DH, (h + 1) * A_DH)
            p = jnp.exp(logit_scr[h, j] - m_scr[h])
            l_scr[h] += p
            acc_scr[h] += _dot(p.astype(BF16), v_ref[pl.ds(r0, LANE), sl])
        return carry

    lax.fori_loop(0, ntiles, pv_body, 0)
    for h in range(A_H):
        l = jnp.sum(l_scr[h], axis=-1, keepdims=True)
        o_ref[:, h * A_DH:(h + 1) * A_DH] = (acc_scr[h] / l).astype(o_ref.dtype)


def _dsa_prompt(slab_b, slab_f, bias_tiles, topk):
    nb = slab_b.shape[0]
    nq = (T_PAD - LANE) // DP_TQ
    nt_all = DP_QT * nq + 1
    wide = A_H * A_DH
    return pl.pallas_call(
        functools.partial(_dsa_prompt_kernel, topk=topk),
        grid=(nb, nq),
        in_specs=[pl.BlockSpec((None, DP_TQ, wide), lambda b, i: (b, i, COL_AQ // wide)),
                  pl.BlockSpec((None, DP_TQ, wide), lambda b, i: (b, i, COL_IQ // wide)),
                  pl.BlockSpec((None, DP_TQ, LANE), lambda b, i: (b, i, COL_SM // LANE)),
                  _resident((None, T_PAD, wide), lambda b, i: (b, 0, COL_AK // wide)),
                  _resident((None, T_PAD, wide), lambda b, i: (b, 0, COL_AV // wide)),
                  _resident((None, T_PAD, LANE), lambda b, i: (b, 0, COL_SM // LANE)),
                  _resident((3, A_H, LANE, LANE), lambda b, i: (0, 0, 0, 0))],
        out_specs=pl.BlockSpec((None, DP_TQ, wide), lambda b, i: (b, i, 0)),
        out_shape=jax.ShapeDtypeStruct((nb, T_PAD - LANE, wide), BF16),
        scratch_shapes=[pltpu.VMEM((nt_all, DP_TQ, LANE), F32),
                        pltpu.VMEM((nt_all, 2 * LANE, 2 * IDX_D), BF16),
                        pltpu.VMEM((IDX_H, DP_TQ, LANE), F32),
                        pltpu.VMEM((A_H, nt_all, DP_TQ, LANE), F32),
                        pltpu.VMEM((A_H, DP_TQ, LANE), F32),
                        pltpu.VMEM((A_H, DP_TQ, LANE), F32),
                        pltpu.VMEM((A_H, DP_TQ, A_DH), F32)],
        compiler_params=_cparams(("arbitrary", "arbitrary")),
        name="dsa_prompt",
    )(slab_b, slab_b, slab_f, slab_b, slab_b, slab_b, bias_tiles)


def _dsa_sample_score_kernel(pt_ref, iq_ref, g_ref, *refs, seq, n_pages):
    page_refs, o_ref, kpad = refs[:n_pages], refs[n_pages], refs[n_pages + 1]

    @pl.when(pl.program_id(0) == 0)
    def _():
        kpad[...] = jnp.zeros_like(kpad)

    row = lax.broadcasted_iota(I32, (seq, LANE), 0)
    lane = lax.broadcasted_iota(I32, (seq, LANE), 1)
    w_all = g_ref[:, SM_IW:SM_IW + IDX_H] * (IDX_D ** -0.5 * IDX_H ** -0.5)
    wb = [jnp.broadcast_to(w_all[:, h:h + 1], (seq, LANE)) for h in range(IDX_H)]
    iqs = jnp.concatenate([iq_ref[:, h * IDX_D:(h + 1) * IDX_D] for h in range(IDX_H)], axis=0).astype(BF16)

    def scores(kt):
        d = _dot_nt(iqs, kt)
        sc = jnp.zeros((seq, LANE), F32)
        for h in range(IDX_H):
            sc = sc + jnp.maximum(d[h * seq:(h + 1) * seq], 0.0) * wb[h]
        return sc

    for p in range(n_pages):
        o_ref[p] = scores(page_refs[p][...].astype(BF16))
    kpad[0:seq, :] = g_ref[:, SM_IK:SM_IK + IDX_D]
    sc = scores(kpad[...].astype(BF16))
    o_ref[n_pages] = jnp.where(lane <= row, sc, -jnp.inf)


def _page_map(p):
    return lambda b, pt: (pt[b, p], 0, 0)


def _dsa_sample_scores(page_table, slab_f, kidx_pool, seq):
    nb, n_pages = page_table.shape
    page_specs = [pl.BlockSpec((None, PAGE, IDX_D), _page_map(p)) for p in range(n_pages)]
    grid_spec = pltpu.PrefetchScalarGridSpec(
        num_scalar_prefetch=1,
        grid=(nb,),
        in_specs=[pl.BlockSpec((seq, IDX_H * IDX_D), lambda b, pt: (b, COL_IQ // 1024)),
                  pl.BlockSpec((seq, LANE), lambda b, pt: (b, COL_SM // LANE))] + page_specs,
        out_specs=pl.BlockSpec((n_pages + 1, seq, LANE), lambda b, pt: (0, b, 0)),
        scratch_shapes=[pltpu.VMEM((LANE, IDX_D), F32)],
    )
    return pl.pallas_call(
        functools.partial(_dsa_sample_score_kernel, seq=seq, n_pages=n_pages),
        grid_spec=grid_spec,
        out_shape=jax.ShapeDtypeStruct((n_pages + 1, nb * seq, LANE), F32),
        compiler_params=_cparams(("arbitrary",)),
        name="dsa_sample_scores",
    )(page_table, slab_f, slab_f, *([kidx_pool] * n_pages))


def _dsa_select_kernel(keys_ref, lo_ref, hi_ref, pos_ref, *, topk):
    lo, hi, tie_pos = _topk_select(keys_ref, keys_ref.shape[0], topk)
    lo_ref[...] = lo
    hi_ref[...] = hi
    pos_ref[...] = tie_pos


def _dsa_select(keys, topk):
    nt, rows, _ = keys.shape
    row_spec = pl.BlockSpec((LANE, LANE), lambda r: (r, 0))
    return pl.pallas_call(
        functools.partial(_dsa_select_kernel, topk=topk),
        grid=(rows // LANE,),
        in_specs=[pl.BlockSpec((nt, LANE, LANE), lambda r: (0, r, 0))],
        out_specs=[row_spec, row_spec, row_spec],
        out_shape=[jax.ShapeDtypeStruct((rows, LANE), F32),
                   jax.ShapeDtypeStruct((rows, LANE), F32),
                   jax.ShapeDtypeStruct((rows, LANE), I32)],
        compiler_params=_cparams(("arbitrary",)),
        name="dsa_select",
    )(keys)


SA_PG = 8


def _dsa_sample_attend_kernel(pt_ref, q_ref, nk_ref, nv_ref, *refs, seq, n_pages):
    kp, vp = refs[:SA_PG], refs[SA_PG:2 * SA_PG]
    keys_ref, lo_ref, hi_ref, pos_ref, bias_ref, o_ref, s_scr, vb, kpad, vpad = refs[2 * SA_PG:]
    b, g = pl.program_id(0), pl.program_id(1)
    ntile = n_pages + 1

    @pl.when((b == 0) & (g == 0))
    def _():
        kpad[...] = jnp.zeros_like(kpad)
        vpad[...] = jnp.zeros_like(vpad)

    qh = [q_ref[:, h * A_DH:(h + 1) * A_DH].astype(BF16) for h in range(A_H)]
    for j in range(SA_PG):
        t = g * SA_PG + j
        r0 = pl.multiple_of(t * PAGE, PAGE)
        for h in range(A_H):
            kh = kp[j][pl.ds(h, PAGE, stride=A_H), :].astype(BF16)
            s_scr[h, t] = _dot_nt(qh[h], kh)
            vb[h, pl.ds(r0, PAGE), :] = vp[j][pl.ds(h, PAGE, stride=A_H), :].astype(BF16)

    @pl.when(g == pl.num_programs(1) - 1)
    def _():
        kpad[0:seq, :] = nk_ref[...]
        vpad[0:seq, :] = nv_ref[...]
        lo, hi, tie_pos = lo_ref[...], hi_ref[...], pos_ref[...]
        sel = jnp.concatenate([_selected(keys_ref[t], t, lo, hi, tie_pos) for t in range(ntile)], axis=1)
        for h in range(A_H):
            sl = slice(h * A_DH, (h + 1) * A_DH)
            s_scr[h, n_pages] = _dot_nt(qh[h], kpad[:, sl].astype(BF16))
            vb[h, n_pages * PAGE:ntile * PAGE, :] = vpad[:, sl].astype(BF16)
            s = jnp.concatenate([s_scr[h, t] for t in range(ntile)], axis=1) * (A_DH ** -0.5) + bias_ref[h]
            s = jnp.where(sel, s, NEG_BIG)
            p = jnp.where(sel, jnp.exp(s - jnp.max(s, axis=-1, keepdims=True)), 0.0)
            out = _dot(p.astype(BF16), vb[h]) / jnp.sum(p, axis=-1, keepdims=True)
            o_ref[:, sl] = out.astype(o_ref.dtype)


def _group_page_map(j):
    return lambda b, g, pt: (pt[b, g * SA_PG + j], 0, 0)


def _dsa_sample_attend(page_table, slab_f, k_pool, v_pool, keys, lo, hi, tie_pos, bias_s, seq):
    nb, n_pages = page_table.shape
    assert n_pages % SA_PG == 0
    wide = A_H * A_DH
    ntile = n_pages + 1
    page_specs = [pl.BlockSpec((None, PAGE * A_H, A_DH), _group_page_map(j)) for j in range(SA_PG)]
    grid_spec = pltpu.PrefetchScalarGridSpec(
        num_scalar_prefetch=1,
        grid=(nb, n_pages // SA_PG),
        in_specs=[pl.BlockSpec((seq, wide), lambda b, g, pt: (b, COL_AQ // wide)),
                  pl.BlockSpec((seq, wide), lambda b, g, pt: (b, COL_AK // wide)),
                  pl.BlockSpec((seq, wide), lambda b, g, pt: (b, COL_AV // wide))]
                 + page_specs + page_specs
                 + [pl.BlockSpec((ntile, seq, LANE), lambda b, g, pt: (0, b, 0)),
                    pl.BlockSpec((seq, LANE), lambda b, g, pt: (b, 0)),
                    pl.BlockSpec((seq, LANE), lambda b, g, pt: (b, 0)),
                    pl.BlockSpec((seq, LANE), lambda b, g, pt: (b, 0)),
                    _resident((A_H, seq, ntile * LANE), lambda b, g, pt: (0, 0, 0))],
        out_specs=pl.BlockSpec((seq, wide), lambda b, g, pt: (b, 0)),
        scratch_shapes=[pltpu.VMEM((A_H, ntile, seq, LANE), F32),
                        pltpu.VMEM((A_H, ntile * PAGE, A_DH), BF16),
                        pltpu.VMEM((PAGE, wide), F32),
                        pltpu.VMEM((PAGE, wide), F32)],
    )
    return pl.pallas_call(
        functools.partial(_dsa_sample_attend_kernel, seq=seq, n_pages=n_pages),
        grid_spec=grid_spec,
        out_shape=jax.ShapeDtypeStruct((nb * seq, wide), F32),
        compiler_params=_cparams(("arbitrary", "arbitrary")),
        name="dsa_sample_attend",
    )(page_table, slab_f, slab_f, slab_f, *([k_pool] * SA_PG), *([v_pool] * SA_PG), keys, lo, hi, tie_pos, bias_s)


MO_TM = 256


def _mixout_kernel(mm_ref, ma_ref, x_ref, wo_ref, g_ref, wq_ref, k1_ref, k2_ref,
                   hp_ref, xnt_ref, s1_ref, s2_ref):
    half = M_H * M_DV
    y = (_dot(mm_ref[...].astype(BF16), wo_ref[0:half, :])
         + _dot(ma_ref[...].astype(BF16), wo_ref[half:, :]))
    hp = x_ref[...] + y
    hp_ref[...] = hp
    xn = hp * lax.rsqrt(jnp.mean(hp * hp, axis=-1, keepdims=True) + RMS_EPS) * g_ref[...]
    xnt = xn.T.astype(BF16)
    xnt_ref[...] = xnt
    qqt = _dot(wq_ref[...], xnt)
    hq = PEER_DQ // 2
    for h in range(PEER_H):
        q1 = qqt[h * PEER_DQ:h * PEER_DQ + hq, :].astype(BF16)
        q2 = qqt[h * PEER_DQ + hq:(h + 1) * PEER_DQ, :].astype(BF16)
        s1_ref[h] = _dot(k1_ref[h], q1)
        s2_ref[h] = _dot(k2_ref[h], q2)


def _mixout(mm, ma, x2d, wo, g2, wqt, k1, k2):
    rows = x2d.shape[0]
    half = M_H * M_DV
    c2 = lambda i: (0, 0)
    c3 = lambda i: (0, 0, 0)
    return pl.pallas_call(
        _mixout_kernel,
        grid=(rows // MO_TM,),
        in_specs=[pl.BlockSpec((MO_TM, half), lambda i: (i, 0)),
                  pl.BlockSpec((MO_TM, A_H * A_DH), lambda i: (i, 0)),
                  pl.BlockSpec((MO_TM, D_MODEL), lambda i: (i, 0)),
                  _resident((half + A_H * A_DH, D_MODEL), c2),
                  _resident((1, D_MODEL), c2),
                  _resident((PEER_H * PEER_DQ, D_MODEL), c2),
                  _resident((PEER_H, N_KEYS, PEER_DQ // 2), c3),
                  _resident((PEER_H, N_KEYS, PEER_DQ // 2), c3)],
        out_specs=[pl.BlockSpec((MO_TM, D_MODEL), lambda i: (i, 0)),
                   pl.BlockSpec((D_MODEL, MO_TM), lambda i: (0, i)),
                   pl.BlockSpec((PEER_H, N_KEYS, MO_TM), lambda i: (0, 0, i)),
                   pl.BlockSpec((PEER_H, N_KEYS, MO_TM), lambda i: (0, 0, i))],
        out_shape=[jax.ShapeDtypeStruct((rows, D_MODEL), F32),
                   jax.ShapeDtypeStruct((D_MODEL, rows), BF16),
                   jax.ShapeDtypeStruct((PEER_H, N_KEYS, rows), F32),
                   jax.ShapeDtypeStruct((PEER_H, N_KEYS, rows), F32)],
        compiler_params=_cparams(("arbitrary",)),
        name="mixout",
    )(mm, ma, x2d, wo, g2, wqt, k1, k2)


PS_TN = 256


def _extract_top(s, count):
    n = s.shape[0]
    idx = lax.broadcasted_iota(I32, s.shape, 0)
    rank = jnp.full(s.shape, float(count), F32)
    rem = s
    vals = []
    for r in range(count):
        mx = jnp.max(rem, axis=0, keepdims=True)
        first = jnp.min(jnp.where(rem == mx, idx, n), axis=0, keepdims=True)
        hit = idx == first
        rank = jnp.where(hit, float(r), rank)
        rem = jnp.where(hit, -jnp.inf, rem)
        vals.append(mx)
    return vals, rank


def _extract_top_distinct(s, count):
    rank = jnp.full(s.shape, float(count), F32)
    rem = s
    vals = []
    for r in range(count):
        mx = jnp.max(rem, axis=0, keepdims=True)
        hit = rem == mx
        rank = jnp.where(hit, float(r), rank)
        rem = jnp.where(hit, -jnp.inf, rem)
        vals.append(mx)
    marked = jnp.sum(jnp.where(rank < float(count), 1.0, 0.0), axis=0, keepdims=True)
    return vals, rank, marked == float(count)


def _extract_top_auto(s, count):
    vals, rank, ok = _extract_top_distinct(s, count)
    any_tie = jnp.min(jnp.where(ok, 1.0, 0.0)) < 0.5
    return lax.cond(any_tie, lambda: tuple(_extract_top(s, count)), lambda: (vals, rank))


_PAIR_COUNT = [PEER_TOPK // (r1 + 1) for r1 in range(PEER_TOPK)]
_PAIR_START = [sum(_PAIR_COUNT[:r1]) for r1 in range(PEER_TOPK)]
_PAIR_ROWS = -(-sum(_PAIR_COUNT) // 8) * 8


def _peer_select_kernel(s1_ref, s2_ref, li_ref, r2_ref, c1_ref, e2_ref):
    s1 = s1_ref[...]
    s2 = s2_ref[...]
    tn = s1.shape[1]
    v1, rank1 = _extract_top_auto(s1, PEER_TOPK)
    v2, rank2 = _extract_top_auto(s2, PEER_TOPK)
    k = lax.broadcasted_iota(I32, (_PAIR_ROWS, tn), 0)
    in_r1 = [(k >= _PAIR_START[r]) & (k < _PAIR_START[r] + _PAIR_COUNT[r]) for r in range(PEER_TOPK)]
    a1 = jnp.full((_PAIR_ROWS, tn), -jnp.inf, F32)
    start = jnp.zeros((_PAIR_ROWS, tn), I32)
    for r in range(PEER_TOPK):
        a1 = jnp.where(in_r1[r], v1[r], a1)
        start = jnp.where(in_r1[r], _PAIR_START[r], start)
    r2_of_row = k - start
    a2 = jnp.zeros((_PAIR_ROWS, tn), F32)
    for r in range(PEER_TOPK):
        a2 = jnp.where(r2_of_row == r, v2[r], a2)
    sc, crank = _extract_top(a1 + a2, PEER_TOPK)
    chosen = jnp.where(crank < float(PEER_TOPK), 1.0, 0.0)
    li = jnp.zeros(s1.shape, F32)
    for r in range(PEER_TOPK):
        cnt = jnp.sum(jnp.where(in_r1[r], chosen, 0.0), axis=0, keepdims=True)
        li = jnp.where(rank1 == float(r), cnt, li)
    z = jnp.zeros_like(sc[0])
    for c in range(PEER_TOPK):
        z = z + jnp.exp(sc[c] - sc[0])
    li_ref[...] = li
    c1_ref[...] = jnp.exp(s1 - v1[0]) / z
    e2 = jnp.exp(s2 - v2[0])
    for c in range(tn // LANE):
        r2_ref[c] = rank2[:, c * LANE:(c + 1) * LANE]
        e2_ref[c] = e2[:, c * LANE:(c + 1) * LANE]


def _peer_select(s1t, s2t):
    nh, nk, rows = s1t.shape
    spec = pl.BlockSpec((None, nk, PS_TN), lambda t, h: (h, 0, t))
    shp = jax.ShapeDtypeStruct((nh, nk, rows), F32)
    spec2 = pl.BlockSpec((None, PS_TN // LANE, nk, LANE), lambda t, h: (h, t, 0, 0))
    shp2 = jax.ShapeDtypeStruct((nh, rows // LANE, nk, LANE), F32)
    return pl.pallas_call(
        _peer_select_kernel,
        grid=(rows // PS_TN, nh),
        in_specs=[spec, spec],
        out_specs=[spec, spec2, spec, spec2],
        out_shape=[shp, shp2, shp, shp2],
        compiler_params=_cparams(("arbitrary", "arbitrary")),
        name="peer_select",
    )(s1t, s2t)


PD_TN = 512
PD_NI = 8
PD_EB = PD_NI * N_KEYS


def _peer_dense_kernel(xnt_ref, u_ref, v_ref, li_ref, r2_ref, c1_ref, e2_ref, hp_ref, g_ref, o_ref,
                       a0, a1, p0, p1):
    s = pl.program_id(1)
    n_blocks = pl.num_programs(1) - 2

    @pl.when(s == 0)
    def _():
        o_ref[...] = jnp.zeros_like(o_ref)
        for ref in (a0, a1, p0, p1):
            ref[...] = jnp.zeros_like(ref)

    live = (s >= 1) & (s <= n_blocks)

    ntc = PD_TN // LANE

    def step(a_cur, a_prev, p_cur, p_prev):
        pt = jnp.concatenate([p_cur[tc] for tc in range(ntc)], axis=1)
        ocw = D_MODEL // PD_NI
        for il in range(PD_NI):
            rows = slice(il * N_KEYS, (il + 1) * N_KEYS)
            ocols = slice(il * ocw, (il + 1) * ocw)
            o_ref[:, ocols] += _dot_tn(pt, v_ref[:, ocols])
            pre = _dot(u_ref[rows, :], xnt_ref[...])
            for tc in range(ntc):
                a_cur[tc, rows, :] = pre[:, tc * LANE:(tc + 1) * LANE]
            ng = PD_NI // (PD_NI // ntc)
            grp, tc = divmod(il, ntc)
            cols = slice(tc * LANE, (tc + 1) * LANE)
            hj = N_KEYS // 2
            for jh in range(2):
                jrows = slice(jh * hj, (jh + 1) * hj)
                w = [jnp.zeros((hj, LANE), F32) for _ in range(ng)]
                for h in range(PEER_H):
                    r2 = r2_ref[h, tc, jrows, :]
                    e2 = e2_ref[h, tc, jrows, :]
                    for q in range(ng):
                        i1 = grp * ng + q
                        li = li_ref[h, i1:i1 + 1, cols]
                        c1 = c1_ref[h, i1:i1 + 1, cols]
                        w[q] = w[q] + jnp.where(r2 < li, e2, 0.0) * c1
                for q in range(ng):
                    erows = slice((grp * ng + q) * N_KEYS + jh * hj, (grp * ng + q) * N_KEYS + (jh + 1) * hj)
                    a = a_prev[tc, erows, :]
                    act = 0.5 * a * (1.0 + lax.erf(a * (2.0 ** -0.5)))
                    p_prev[tc, erows, :] = jnp.where(live, w[q] * act, 0.0).astype(BF16)

    parity = lax.rem(s, 2)
    pl.when(parity == 0)(lambda: step(a0, a1, p0, p1))
    pl.when(parity == 1)(lambda: step(a1, a0, p1, p0))

    @pl.when(s == pl.num_programs(1) - 1)
    def _():
        h = hp_ref[...] + o_ref[...]
        o_ref[...] = h * lax.rsqrt(jnp.mean(h * h, axis=-1, keepdims=True) + RMS_EPS) * g_ref[...]


def _peer_dense(xnt, u, v, li, r2, c1, e2, hp, gf):
    rows = xnt.shape[1]
    n_blocks = N_EXPERTS // PD_EB

    def blk(s, lag):
        return jnp.clip(s - lag, 0, n_blocks - 1)

    key2_spec = pl.BlockSpec((PEER_H, PD_TN // LANE, N_KEYS, LANE), lambda t, s: (0, t, 0, 0))
    key1_spec = pl.BlockSpec((PEER_H, PD_NI, PD_TN), lambda t, s: (0, blk(s, 1), t))
    return pl.pallas_call(
        _peer_dense_kernel,
        grid=(rows // PD_TN, n_blocks + 2),
        in_specs=[pl.BlockSpec((D_MODEL, PD_TN), lambda t, s: (0, t)),
                  pl.BlockSpec((PD_EB, D_MODEL), lambda t, s: (blk(s, 0), 0)),
                  pl.BlockSpec((PD_EB, D_MODEL), lambda t, s: (blk(s, 2), 0)),
                  key1_spec, key2_spec, key1_spec, key2_spec,
                  pl.BlockSpec((PD_TN, D_MODEL), lambda t, s: (t, 0)),
                  pl.BlockSpec((1, D_MODEL), lambda t, s: (0, 0))],
        out_specs=pl.BlockSpec((PD_TN, D_MODEL), lambda t, s: (t, 0)),
        out_shape=jax.ShapeDtypeStruct((rows, D_MODEL), F32),
        scratch_shapes=[pltpu.VMEM((PD_TN // LANE, PD_EB, LANE), F32), pltpu.VMEM((PD_TN // LANE, PD_EB, LANE), F32),
                        pltpu.VMEM((PD_TN // LANE, PD_EB, LANE), BF16), pltpu.VMEM((PD_TN // LANE, PD_EB, LANE), BF16)],
        compiler_params=_cparams(("arbitrary", "arbitrary")),
        name="peer_dense",
    )(xnt, u, v, li, r2, c1, e2, hp, gf)


def _t5_bucket_table(n):
    rel = np.arange(n)
    max_exact = REL_BUCKETS // 2
    nf = np.maximum(rel, 1).astype(np.float32)
    large = max_exact + (np.log(nf / np.float32(max_exact)) / np.float32(math.log(REL_MAX_DIST / max_exact))
                         * np.float32(REL_BUCKETS - max_exact)).astype(np.int32)
    large = np.minimum(large, REL_BUCKETS - 1)
    return np.where(rel < max_exact, rel, large).astype(np.int32)


def _bias_tiles(rel_bias):
    table = _t5_bucket_table(3 * LANE)
    r = np.arange(LANE)[:, None]
    c = np.arange(LANE)[None, :]
    rel = np.stack([np.maximum(r - c, 0), LANE + r - c, np.full((LANE, LANE), 2 * LANE)])
    assert table[LANE + 1] == REL_BUCKETS - 1
    buckets = table[rel]
    return _bucket_lookup(rel_bias, buckets[:, None])


def _bucket_lookup(rel_bias, buckets):
    tb = rel_bias.astype(F32)
    b = jnp.asarray(buckets)
    hshape = (1, A_H) + (1,) * (buckets.ndim - 2)
    out = jnp.zeros(buckets.shape[:1] + (A_H,) + buckets.shape[2:], F32)
    for k in range(REL_BUCKETS):
        out = jnp.where(b == k, tb[k].reshape(hshape), out)
    return out


def _sample_bias(rel_bias, past, seq, ntile):
    t = np.arange(seq)[:, None]
    c = np.arange(ntile * LANE)[None, :]
    buckets = _t5_bucket_table(past + seq)[np.maximum(past + t - c, 0)]
    return _bucket_lookup(rel_bias, buckets[None, None])[0]


def _relayout_w_in(w):
    cuts = np.cumsum(IN_SPLITS)[:-1].tolist()
    mq, mk, mv, mo, ig, fg, aq, ak, av, iq, ik, iw = jnp.split(w, cuts, axis=-1)
    pad = jnp.zeros((w.shape[0], P_PAD - COL_SM - IDX_D - IDX_H - 2 * M_H), w.dtype)
    return jnp.concatenate([aq, ak, av, iq, mv, mo, mq, mk, ik, iw, ig, fg, pad], axis=-1).astype(BF16)


def kernel(x_prompt, x_sample, cache_k, cache_v, cache_kidx, state_C, state_n, state_m, page_table,
           meta_tokens, rel_bias, final_norm_g, norm1_g, w_in, b_i, b_f, mh_norm_g, w_out, norm2_g,
           peer_wq, peer_k1, peer_k2, peer_u, peer_v):
    B, S_p, D = x_prompt.shape
    Bs, Ts, _ = x_sample.shape
    n_pages = page_table.shape[1]
    depth = w_in.shape[0]
    assert depth == 1 and D == D_MODEL and N_META + S_p == T_PAD - META_OFF and S_p % LANE == 0
    assert n_pages * PAGE + Ts <= (n_pages + 1) * PAGE and cache_k.shape[2] == PAGE
    topk_p = min(TOPK_MAX, S_p // 4)
    topk_s = min(TOPK_MAX, (n_pages * PAGE + Ts) // 4)
    l = 0

    w1 = _relayout_w_in(w_in[l])
    g1 = norm1_g[l].astype(F32).reshape(1, D)
    g2 = norm2_g[l].astype(F32).reshape(1, D)
    gf = final_norm_g.astype(F32).reshape(1, D)
    zpad = jnp.zeros((LANE - SM_FG - M_H,), F32)
    bias_row = jnp.concatenate([jnp.zeros((SM_IG,), F32), b_i[l].astype(F32), b_f[l].astype(F32), zpad]).reshape(1, LANE)
    mhg = mh_norm_g[l].astype(F32).reshape(1, M_H * M_DV)
    wo = w_out[l].astype(BF16)
    wqt = peer_wq[l].T.astype(BF16)
    k1 = peer_k1[l].astype(BF16)
    k2 = peer_k2[l].astype(BF16)
    u = peer_u[l].astype(BF16)
    v = peer_v[l].astype(BF16)
    bias_tiles = _bias_tiles(rel_bias)

    meta = jnp.broadcast_to(meta_tokens.astype(F32)[None], (B, N_META, D))
    xp_all = jnp.concatenate([x_prompt, jnp.zeros((B, META_OFF, D), F32), meta], axis=1)

    pf, pb = _inproj(xp_all.reshape(B * T_PAD, D), g1, w1)
    pf = pf.reshape(B, T_PAD, P_PAD)
    pb = pb.reshape(B, T_PAD, P_PAD)
    mm_p, c_p, n_p, m_p = _mlstm_prompt(pb, pf, bias_row, mhg)
    ma_p = _dsa_prompt(pb, pf, bias_tiles, topk_p)
    xp2 = x_prompt.reshape(B * S_p, D)
    hp_p, xnt_p, s1_p, s2_p = _mixout(mm_p.reshape(B * S_p, -1), ma_p.reshape(B * S_p, -1), xp2, wo, g2, wqt, k1, k2)
    y_prompt = _peer_dense(xnt_p, u, v, *_peer_select(s1_p, s2_p), hp_p, gf).reshape(B, S_p, D)

    xs2 = x_sample.reshape(Bs * Ts, D)
    sf, _ = _inproj(xs2, g1, w1)
    c0 = state_C[l].astype(F32)
    n0 = state_n[l].astype(F32).reshape(Bs, M_H, 1, M_DK)
    m0 = jnp.broadcast_to(state_m[l].astype(F32)[:, :, None, None], (Bs, M_H, 1, LANE))
    mm_s, c_s, n_s, m_s = _mlstm_sample(sf, bias_row, mhg, c0, n0, m0, Ts)
    n_pool = cache_k.shape[1]
    keys_s = _dsa_sample_scores(page_table, sf, cache_kidx[l].reshape(n_pool, PAGE, IDX_D), Ts)
    lo_s, hi_s, pos_s = _dsa_select(keys_s, topk_s)
    bias_s = _sample_bias(rel_bias, n_pages * PAGE, Ts, n_pages + 1)
    ma_s = _dsa_sample_attend(page_table, sf, cache_k[l].reshape(n_pool, PAGE * A_H, A_DH),
                              cache_v[l].reshape(n_pool, PAGE * A_H, A_DH), keys_s, lo_s, hi_s, pos_s, bias_s, Ts)
    hp_s, xnt_s, s1_s, s2_s = _mixout(mm_s, ma_s, xs2, wo, g2, wqt, k1, k2)
    y_sample = _peer_dense(xnt_s, u, v, *_peer_select(s1_s, s2_s), hp_s, gf).reshape(Bs, Ts, D)

    def seq_rows(a):
        return jnp.concatenate([a[:, T_PAD - N_META:], a[:, :S_p]], axis=1)

    wide = A_H * A_DH
    T = N_META + S_p
    k_prompt = seq_rows(pf[:, :, COL_AK:COL_AK + wide]).reshape(1, B, T, A_H, A_DH)
    v_prompt = seq_rows(pf[:, :, COL_AV:COL_AV + wide]).reshape(1, B, T, A_H, A_DH)
    kidx_prompt = seq_rows(pf[:, :, COL_SM + SM_IK:COL_SM + SM_IK + IDX_D]).reshape(1, B, T, IDX_D)
    k_sample = sf[:, COL_AK:COL_AK + wide].reshape(1, Bs, Ts, A_H, A_DH)
    v_sample = sf[:, COL_AV:COL_AV + wide].reshape(1, Bs, Ts, A_H, A_DH)
    kidx_sample = sf[:, COL_SM + SM_IK:COL_SM + SM_IK + IDX_D].reshape(1, Bs, Ts, IDX_D)
    return (y_prompt, y_sample,
            k_prompt, v_prompt, kidx_prompt,
            c_p[None], n_p[:, :, 0, :][None], m_p[:, :, 0, 0][None],
            k_sample, v_sample, kidx_sample,
            c_s[None], n_s[:, :, 0, :][None], m_s[:, :, 0, 0][None])
```

```python
import functools
import math

import numpy as np
import jax
import jax.numpy as jnp
from jax import lax
from jax.experimental import pallas as pl
from jax.experimental.pallas import tpu as pltpu

F32 = jnp.float32
BF16 = jnp.bfloat16
I32 = jnp.int32

D_MODEL = 2048
N_META = 16
RMS_EPS = 1e-6
M_H, M_DK, M_DV = 4, 128, 256
A_H, A_DH = 8, 128
IDX_H, IDX_D = 16, 64
TOPK_MAX = 256
REL_BUCKETS, REL_MAX_DIST = 32, 128
PEER_H, PEER_DQ, N_KEYS, PEER_TOPK = 8, 256, 128, 16
N_EXPERTS = N_KEYS * N_KEYS
PAGE = 128
IN_SPLITS = (M_H * M_DK, M_H * M_DK, M_H * M_DV, M_H * M_DV, M_H, M_H,
             A_H * A_DH, A_H * A_DH, A_H * A_DH, IDX_H * IDX_D, IDX_D, IDX_H)

LANE = 128
VMEM_LIMIT = 56 * 1024 * 1024

COL_AQ, COL_AK, COL_AV, COL_IQ, COL_MV, COL_MO, COL_MQ, COL_MK, COL_SM = (
    0, 1024, 2048, 3072, 4096, 5120, 6144, 6656, 7168)
P_PAD = COL_SM + 512
SM_IK, SM_IW, SM_IG, SM_FG = 0, 64, 80, 84
T_PAD = 17 * LANE
META_TILE = 16
META_OFF = LANE - N_META
INT_MIN = -(2 ** 31)
NEG_BIG = -1e30


def _cparams(sem, vmem=VMEM_LIMIT):
    return pltpu.CompilerParams(dimension_semantics=sem, vmem_limit_bytes=vmem)


def _resident(shape, index_map):
    return pl.BlockSpec(shape, index_map, pipeline_mode=pl.Buffered(1))


def _dot(a, b):
    return jnp.dot(a, b, preferred_element_type=F32)


def _dot_nt(a, b):
    return lax.dot_general(a, b, (((1,), (1,)), ((), ())), preferred_element_type=F32)


def _dot_tn(a, b):
    return lax.dot_general(a, b, (((0,), (0,)), ((), ())), preferred_element_type=F32)


IP_TM_MAX, IP_TN = 1088, 768


def _inproj_kernel(x_ref, g_ref, w_ref, of_ref, ob_ref, hn_ref):
    @pl.when(pl.program_id(1) == 0)
    def _():
        x = x_ref[...]
        y = x * lax.rsqrt(jnp.mean(x * x, axis=-1, keepdims=True) + RMS_EPS) * g_ref[...]
        hn_ref[...] = y.astype(BF16)

    acc = _dot(hn_ref[...], w_ref[...])
    of_ref[...] = acc
    ob_ref[...] = acc.astype(BF16)


def _inproj(x2d, g, w):
    rows = x2d.shape[0]
    tm = IP_TM_MAX if rows % IP_TM_MAX == 0 else min(rows, 1024)
    assert rows % tm == 0
    grid = (rows // tm, P_PAD // IP_TN)
    return pl.pallas_call(
        _inproj_kernel,
        grid=grid,
        in_specs=[pl.BlockSpec((tm, D_MODEL), lambda i, j: (i, 0)),
                  pl.BlockSpec((1, D_MODEL), lambda i, j: (0, 0)),
                  pl.BlockSpec((D_MODEL, IP_TN), lambda i, j: (0, j))],
        out_specs=[pl.BlockSpec((tm, IP_TN), lambda i, j: (i, j)),
                   pl.BlockSpec((tm, IP_TN), lambda i, j: (i, j))],
        out_shape=[jax.ShapeDtypeStruct((rows, P_PAD), F32),
                   jax.ShapeDtypeStruct((rows, P_PAD), BF16)],
        scratch_shapes=[pltpu.VMEM((tm, D_MODEL), BF16)],
        compiler_params=_cparams(("arbitrary", "arbitrary")),
        name="inproj",
    )(x2d, g, w)


def _mlstm_gates(gates, bias_row, lo, hi):
    row = lax.broadcasted_iota(I32, (LANE, LANE), 0)
    col = lax.broadcasted_iota(I32, (LANE, LANE), 1)
    valid = (row >= lo) & (row < hi)
    pre = gates + bias_row
    li = jnp.where(valid, pre, -jnp.inf)
    lf = jnp.where(valid, jax.nn.log_sigmoid(pre), 0.0)
    tril = jnp.where(col <= row, 1.0, 0.0).astype(F32)
    b = jnp.dot(tril, lf, preferred_element_type=F32, precision=lax.Precision.HIGHEST)
    return li, b, li.T, b.T


def _mlstm_head(q, k, v, li, b, lit, bt, h, c_old, n_old, m_old):
    li_col = li[:, SM_IG + h:SM_IG + h + 1]
    b_col = b[:, SM_FG + h:SM_FG + h + 1]
    li_row = lit[SM_IG + h:SM_IG + h + 1, :]
    b_row = bt[SM_FG + h:SM_FG + h + 1, :]
    row = lax.broadcasted_iota(I32, (LANE, LANE), 0)
    col = lax.broadcasted_iota(I32, (LANE, LANE), 1)
    dmat = jnp.where(col <= row, b_col - b_row + li_row, -jnp.inf)
    inter = b_col + m_old
    m_t = jnp.maximum(inter, jnp.max(dmat, axis=-1, keepdims=True))
    w_inter = jnp.exp(inter - m_t)
    scale = M_DK ** -0.5
    s = _dot_nt(q, k) * scale * jnp.exp(dmat - m_t)
    num = w_inter * _dot(q, c_old.astype(BF16)) + _dot(s.astype(BF16), v)
    qf = q.astype(F32)
    den = w_inter * jnp.sum(qf * n_old, axis=-1, keepdims=True) + jnp.sum(s, axis=-1, keepdims=True)
    h_out = num / jnp.maximum(jnp.abs(den), jnp.exp(-m_t))
    b_last = b_col[LANE - 1:LANE, :]
    g_col = b_last - b_col + li_col
    m_new = jnp.maximum(b_last + m_old, jnp.max(g_col, axis=0, keepdims=True))
    a = jnp.exp(b_last + m_old - m_new)
    wk = jnp.exp(g_col - m_new)
    wv = (wk * v.astype(F32)).astype(BF16)
    c_new = a * c_old + scale * _dot_tn(k, wv)
    n_new = a * n_old + scale * jnp.sum(wk * k.astype(F32), axis=0, keepdims=True)
    return h_out, c_new, n_new, m_new


def _mlstm_mix(h_out, mo, g):
    hn = h_out * lax.rsqrt(jnp.mean(h_out * h_out, axis=-1, keepdims=True) + RMS_EPS)
    return hn * g * jax.nn.sigmoid(mo)


def _mlstm_prompt_kernel(q_ref, k_ref, v_ref, mo_ref, g_ref, bias_ref, mhg_ref,
                         out_ref, c_out, n_out, m_out, c_scr, n_scr, m_scr):
    c = pl.program_id(1)

    @pl.when(c == 0)
    def _():
        c_scr[...] = jnp.zeros_like(c_scr)
        n_scr[...] = jnp.zeros_like(n_scr)
        m_scr[...] = jnp.zeros_like(m_scr)

    lo = jnp.where(c == 0, META_OFF, 0)
    li, b, lit, bt = _mlstm_gates(g_ref[...], bias_ref[...], lo, LANE)
    for h in range(M_H):
        q = q_ref[:, h * M_DK:(h + 1) * M_DK]
        k = k_ref[:, h * M_DK:(h + 1) * M_DK]
        v = v_ref[:, h * M_DV:(h + 1) * M_DV]
        h_out, c_new, n_new, m_new = _mlstm_head(
            q, k, v, li, b, lit, bt, h, c_scr[h], n_scr[h], m_scr[h][:, 0:1])
        c_scr[h] = c_new
        n_scr[h] = n_new
        m_scr[h] = jnp.broadcast_to(m_new, (1, LANE))
        sl = slice(h * M_DV, (h + 1) * M_DV)
        out_ref[:, sl] = _mlstm_mix(h_out, mo_ref[:, sl], mhg_ref[:, sl]).astype(out_ref.dtype)

    @pl.when(c == pl.num_programs(1) - 1)
    def _():
        c_out[...] = c_scr[...]
        n_out[...] = n_scr[...]
        m_out[...] = m_scr[...]


def _mlstm_prompt(slab_b, slab_f, bias_row, mhg):
    nb = slab_b.shape[0]
    nchunk = T_PAD // LANE

    def tile(c):
        return (c + META_TILE) % nchunk

    return pl.pallas_call(
        _mlstm_prompt_kernel,
        grid=(nb, nchunk),
        in_specs=[pl.BlockSpec((None, LANE, M_H * M_DK), lambda b, c: (b, tile(c), COL_MQ // 512)),
                  pl.BlockSpec((None, LANE, M_H * M_DK), lambda b, c: (b, tile(c), COL_MK // 512)),
                  pl.BlockSpec((None, LANE, M_H * M_DV), lambda b, c: (b, tile(c), COL_MV // 1024)),
                  pl.BlockSpec((None, LANE, M_H * M_DV), lambda b, c: (b, tile(c), COL_MO // 1024)),
                  pl.BlockSpec((None, LANE, LANE), lambda b, c: (b, tile(c), COL_SM // LANE)),
                  pl.BlockSpec((1, LANE), lambda b, c: (0, 0)),
                  pl.BlockSpec((1, M_H * M_DV), lambda b, c: (0, 0))],
        out_specs=[pl.BlockSpec((None, LANE, M_H * M_DV), lambda b, c: (b, jnp.maximum(c - 1, 0), 0)),
                   pl.BlockSpec((None, M_H, M_DK, M_DV), lambda b, c: (b, 0, 0, 0)),
                   pl.BlockSpec((None, M_H, 1, M_DK), lambda b, c: (b, 0, 0, 0)),
                   pl.BlockSpec((None, M_H, 1, LANE), lambda b, c: (b, 0, 0, 0))],
        out_shape=[jax.ShapeDtypeStruct((nb, T_PAD - LANE, M_H * M_DV), BF16),
                   jax.ShapeDtypeStruct((nb, M_H, M_DK, M_DV), F32),
                   jax.ShapeDtypeStruct((nb, M_H, 1, M_DK), F32),
                   jax.ShapeDtypeStruct((nb, M_H, 1, LANE), F32)],
        scratch_shapes=[pltpu.VMEM((M_H, M_DK, M_DV), F32),
                        pltpu.VMEM((M_H, 1, M_DK), F32),
                        pltpu.VMEM((M_H, 1, LANE), F32)],
        compiler_params=_cparams(("arbitrary", "arbitrary")),
        name="mlstm_prompt",
    )(slab_b, slab_b, slab_b, slab_f, slab_f, bias_row, mhg)


MS_NB = 2


def _mlstm_sample_kernel(q_ref, k_ref, v_ref, mo_ref, g_ref, bias_ref, mhg_ref, c_in, n_in, m_in,
                         out_ref, c_out, n_out, m_out, qp, kp, vp, gp, *, seq):
    @pl.when(pl.program_id(0) == 0)
    def _():
        qp[...] = jnp.zeros_like(qp)
        kp[...] = jnp.zeros_like(kp)
        vp[...] = jnp.zeros_like(vp)
        gp[...] = jnp.zeros_like(gp)

    for s in range(MS_NB):
        rows = slice(s * seq, (s + 1) * seq)
        qp[0:seq, :] = q_ref[rows, :]
        kp[0:seq, :] = k_ref[rows, :]
        vp[0:seq, :] = v_ref[rows, :]
        gp[0:seq, :] = g_ref[rows, :]
        li, b, lit, bt = _mlstm_gates(gp[...], bias_ref[...], 0, seq)
        for h in range(M_H):
            q = qp[:, h * M_DK:(h + 1) * M_DK].astype(BF16)
            k = kp[:, h * M_DK:(h + 1) * M_DK].astype(BF16)
            v = vp[:, h * M_DV:(h + 1) * M_DV].astype(BF16)
            h_out, c_new, n_new, m_new = _mlstm_head(
                q, k, v, li, b, lit, bt, h, c_in[s, h], n_in[s, h], m_in[s, h][:, 0:1])
            c_out[s, h] = c_new
            n_out[s, h] = n_new
            m_out[s, h] = jnp.broadcast_to(m_new, (1, LANE))
            sl = slice(h * M_DV, (h + 1) * M_DV)
            mixed = _mlstm_mix(h_out[0:seq], mo_ref[rows, sl], mhg_ref[:, sl])
            out_ref[rows, sl] = mixed.astype(out_ref.dtype)


def _mlstm_sample(slab_f, bias_row, mhg, c0, n0, m0, seq):
    nb = c0.shape[0]
    rb = MS_NB * seq
    st4 = lambda i: (i, 0, 0, 0)
    return pl.pallas_call(
        functools.partial(_mlstm_sample_kernel, seq=seq),
        grid=(nb // MS_NB,),
        in_specs=[pl.BlockSpec((rb, M_H * M_DK), lambda i: (i, COL_MQ // 512)),
                  pl.BlockSpec((rb, M_H * M_DK), lambda i: (i, COL_MK // 512)),
                  pl.BlockSpec((rb, M_H * M_DV), lambda i: (i, COL_MV // 1024)),
                  pl.BlockSpec((rb, M_H * M_DV), lambda i: (i, COL_MO // 1024)),
                  pl.BlockSpec((rb, LANE), lambda i: (i, COL_SM // LANE)),
                  pl.BlockSpec((1, LANE), lambda i: (0, 0)),
                  pl.BlockSpec((1, M_H * M_DV), lambda i: (0, 0)),
                  pl.BlockSpec((MS_NB, M_H, M_DK, M_DV), st4),
                  pl.BlockSpec((MS_NB, M_H, 1, M_DK), st4),
                  pl.BlockSpec((MS_NB, M_H, 1, LANE), st4)],
        out_specs=[pl.BlockSpec((rb, M_H * M_DV), lambda i: (i, 0)),
                   pl.BlockSpec((MS_NB, M_H, M_DK, M_DV), st4),
                   pl.BlockSpec((MS_NB, M_H, 1, M_DK), st4),
                   pl.BlockSpec((MS_NB, M_H, 1, LANE), st4)],
        out_shape=[jax.ShapeDtypeStruct((nb * seq, M_H * M_DV), BF16),
                   jax.ShapeDtypeStruct((nb, M_H, M_DK, M_DV), F32),
                   jax.ShapeDtypeStruct((nb, M_H, 1, M_DK), F32),
                   jax.ShapeDtypeStruct((nb, M_H, 1, LANE), F32)],
        scratch_shapes=[pltpu.VMEM((LANE, M_H * M_DK), F32),
                        pltpu.VMEM((LANE, M_H * M_DK), F32),
                        pltpu.VMEM((LANE, M_H * M_DV), F32),
                        pltpu.VMEM((LANE, LANE), F32)],
        compiler_params=_cparams(("arbitrary",)),
        name="mlstm_sample",
    )(slab_f, slab_f, slab_f, slab_f, slab_f, bias_row, mhg, c0, n0, m0)


def _sortable(x):
    x = jnp.where(x == 0.0, 0.0, x)
    bits = lax.bitcast_convert_type(x, I32)
    return bits ^ ((bits >> 31) & jnp.int32(0x7FFFFFFF))


def _unsortable(key):
    return lax.bitcast_convert_type(key ^ ((key >> 31) & jnp.int32(0x7FFFFFFF)), F32)


def _row_reduce(s_ref, ntiles, fn, init, reduce):
    shape = s_ref.shape[1:]
    acc = lax.fori_loop(0, ntiles, lambda j, a: fn(a, j, s_ref[j]), jnp.full(shape, init, F32))
    return jnp.broadcast_to(reduce(acc, axis=-1, keepdims=True), shape)


def _count(s_ref, ntiles, pred):
    return _row_reduce(s_ref, ntiles, lambda a, j, x: a + jnp.where(pred(j, x), 1.0, 0.0), 0.0, jnp.sum)


TOPK_BISECTIONS = 32


def _topk_select(s_ref, ntiles, topk):
    shape = s_ref.shape[1:]
    inf = jnp.inf
    vmax = _row_reduce(s_ref, ntiles, lambda a, j, x: jnp.maximum(a, x), -inf, jnp.max)
    vmin = _row_reduce(s_ref, ntiles, lambda a, j, x: jnp.minimum(a, jnp.where(x > -inf, x, inf)), inf, jnp.min)
    n_valid = _count(s_ref, ntiles, lambda j, x: x > -inf)
    k = jnp.minimum(float(topk), n_valid)

    def in_bracket(x, lo, hi):
        return (x >= lo) & (x < hi) & (x > -inf)

    top_tied = _count(s_ref, ntiles, lambda j, x: x >= vmax) >= k
    lo = jnp.where(top_tied, vmax, vmin)
    hi = jnp.where(top_tied, inf, vmax)

    def halve(_, c):
        lo, hi = c
        mid = lo * 0.5 + hi * 0.5
        ge = _count(s_ref, ntiles, lambda j, x: x >= mid) >= k
        return jnp.where(ge, mid, lo), jnp.where(ge, hi, mid)

    lo, hi = lax.fori_loop(0, TOPK_BISECTIONS, halve, (lo, hi))
    mid = lo * 0.5 + hi * 0.5
    adjacent = (mid == lo) | (mid == hi)

    def surplus(lo, hi):
        need = k - _count(s_ref, ntiles, lambda j, x: x >= hi)
        return need, _count(s_ref, ntiles, lambda j, x: in_bracket(x, lo, hi)) - need

    need, excess = surplus(lo, hi)
    unresolved = jnp.max(jnp.where((excess > 0.5) & ~adjacent, 1.0, 0.0)) > 0.5

    def by_bits():
        def bit_body(it, thr):
            cand = thr + lax.shift_left(jnp.int32(1), 31 - it)
            cnt = _count(s_ref, ntiles, lambda j, x: _sortable(x) >= cand)
            return jnp.where(cnt >= k, cand, thr)

        kth = _unsortable(lax.fori_loop(0, 32, bit_body, jnp.full(shape, INT_MIN, I32)))
        above = _row_reduce(s_ref, ntiles, lambda a, j, x: jnp.minimum(a, jnp.where(x > kth, x, inf)), inf, jnp.min)
        return (kth, above) + surplus(kth, above)

    lo, hi, need, excess = lax.cond(unresolved, by_bits, lambda: (lo, hi, need, excess))
    lane = lax.broadcasted_iota(I32, shape, len(shape) - 1)
    nbits = max(1, int(s_ref.shape[0] * LANE - 1).bit_length())

    def tie_fn():
        def pos_body(it, qp):
            cand = qp + lax.shift_left(jnp.int32(1), nbits - 1 - it)
            cnt = _count(s_ref, ntiles, lambda j, x: in_bracket(x, lo, hi) & (j * LANE + lane < cand))
            return jnp.where(cnt <= need - 1.0, cand, qp)

        return lax.fori_loop(0, nbits, pos_body, jnp.zeros(shape, I32))

    has_tie = jnp.max(excess) > 0.5
    qpos = lax.cond(has_tie, tie_fn, lambda: jnp.full(shape, 2 ** nbits - 1, I32))
    return lo, hi, qpos


def _selected(x, j, lo, hi, qpos):
    lane = lax.broadcasted_iota(I32, x.shape, len(x.shape) - 1)
    return (x >= hi) | ((x >= lo) & (x < hi) & (x > -jnp.inf) & (j * LANE + lane <= qpos))


DP_QT = 2
DP_TQ = DP_QT * LANE


def _dsa_prompt_kernel(q_ref, iq_ref, g_ref, k_ref, v_ref, ki_ref, bias_ref, o_ref,
                       keys_scr, ki2_scr, wb_scr, logit_scr, m_scr, l_scr, acc_scr, *, topk):
    i = pl.program_id(1)
    ntiles = DP_QT * (i + 1) + 1
    nt_all = keys_scr.shape[0]
    row = lax.broadcasted_iota(I32, (DP_TQ, LANE), 0)
    lane = lax.broadcasted_iota(I32, (DP_TQ, LANE), 1)
    qpos = N_META + DP_TQ * i + row

    def phys_row(j):
        return pl.multiple_of(jnp.where(j == 0, META_TILE, j - 1) * LANE, LANE)

    @pl.when(i == 0)
    def _():
        z = jnp.zeros((LANE, IDX_D), F32)
        for j in range(nt_all):
            phys = META_TILE if j == 0 else j - 1
            kt = ki_ref[phys * LANE:(phys + 1) * LANE, SM_IK:SM_IK + IDX_D].astype(F32)
            both = jnp.concatenate([jnp.concatenate([kt, z], axis=1), jnp.concatenate([z, kt], axis=1)], axis=0)
            ki2_scr[j] = both.astype(BF16)

    w_all = g_ref[:, SM_IW:SM_IW + IDX_H] * (IDX_D ** -0.5 * IDX_H ** -0.5)
    for h in range(IDX_H):
        wb_scr[h] = jnp.broadcast_to(w_all[:, h:h + 1], (DP_TQ, LANE))

    def score_body(j, carry):
        k2 = ki2_scr[j]
        sc = jnp.zeros((DP_TQ, LANE), F32)
        for hp in range(IDX_H // 2):
            d2 = _dot_nt(iq_ref[:, hp * LANE:(hp + 1) * LANE], k2)
            sc = (sc + jnp.maximum(d2[:, :LANE], 0.0) * wb_scr[2 * hp]
                  + jnp.maximum(d2[:, LANE:], 0.0) * wb_scr[2 * hp + 1])
        kpos = N_META + LANE * (j - 1) + lane
        valid = (kpos >= 0) & (kpos <= qpos)
        keys_scr[j] = jnp.where(valid, sc, -jnp.inf)
        return carry

    lax.fori_loop(0, ntiles, score_body, 0)
    lo, hi, tie_pos = _topk_select(keys_scr, ntiles, topk)

    m_scr[...] = jnp.full(m_scr.shape, NEG_BIG, F32)

    def qk_body(j, carry):
        r0 = phys_row(j)
        sel = _selected(keys_scr[j], j, lo, hi, tie_pos)
        dsel = [jnp.clip(DP_QT * i + sub - (j - 1), 0, 2) for sub in range(DP_QT)]
        for h in range(A_H):
            sl = slice(h * A_DH, (h + 1) * A_DH)
            bias = jnp.concatenate([bias_ref[d, h] for d in dsel], axis=0)
            s = _dot_nt(q_ref[:, sl], k_ref[pl.ds(r0, LANE), sl]) * (A_DH ** -0.5) + bias
            s = jnp.where(sel, s, NEG_BIG)
            logit_scr[h, j] = s
            m_scr[h] = jnp.maximum(m_scr[h], s)
        return carry

    lax.fori_loop(0, ntiles, qk_body, 0)
    for h in range(A_H):
        m_scr[h] = jnp.broadcast_to(jnp.max(m_scr[h], axis=-1, keepdims=True), (DP_TQ, LANE))
    l_scr[...] = jnp.zeros_like(l_scr)
    acc_scr[...] = jnp.zeros_like(acc_scr)

    def pv_body(j, carry):
        r0 = phys_row(j)
        for h in range(A_H):
            sl = slice(h * A_DH, (h + 1) * A_DH)
            p = jnp.exp(logit_scr[h, j] - m_scr[h])
            l_scr[h] += p
            acc_scr[h] += _dot(p.astype(BF16), v_ref[pl.ds(r0, LANE), sl])
        return carry

    lax.fori_loop(0, ntiles, pv_body, 0)
    for h in range(A_H):
        l = jnp.sum(l_scr[h], axis=-1, keepdims=True)
        o_ref[:, h * A_DH:(h + 1) * A_DH] = (acc_scr[h] / l).astype(o_ref.dtype)


def _dsa_prompt(slab_b, slab_f, bias_tiles, topk):
    nb = slab_b.shape[0]
    nq = (T_PAD - LANE) // DP_TQ
    nt_all = DP_QT * nq + 1
    wide = A_H * A_DH
    return pl.pallas_call(
        functools.partial(_dsa_prompt_kernel, topk=topk),
        grid=(nb, nq),
        in_specs=[pl.BlockSpec((None, DP_TQ, wide), lambda b, i: (b, i, COL_AQ // wide)),
                  pl.BlockSpec((None, DP_TQ, wide), lambda b, i: (b, i, COL_IQ // wide)),
                  pl.BlockSpec((None, DP_TQ, LANE), lambda b, i: (b, i, COL_SM // LANE)),
                  _resident((None, T_PAD, wide), lambda b, i: (b, 0, COL_AK // wide)),
                  _resident((None, T_PAD, wide), lambda b, i: (b, 0, COL_AV // wide)),
                  _resident((None, T_PAD, LANE), lambda b, i: (b, 0, COL_SM // LANE)),
                  _resident((3, A_H, LANE, LANE), lambda b, i: (0, 0, 0, 0))],
        out_specs=pl.BlockSpec((None, DP_TQ, wide), lambda b, i: (b, i, 0)),
        out_shape=jax.ShapeDtypeStruct((nb, T_PAD - LANE, wide), BF16),
        scratch_shapes=[pltpu.VMEM((nt_all, DP_TQ, LANE), F32),
                        pltpu.VMEM((nt_all, 2 * LANE, 2 * IDX_D), BF16),
                        pltpu.VMEM((IDX_H, DP_TQ, LANE), F32),
                        pltpu.VMEM((A_H, nt_all, DP_TQ, LANE), F32),
                        pltpu.VMEM((A_H, DP_TQ, LANE), F32),
                        pltpu.VMEM((A_H, DP_TQ, LANE), F32),
                        pltpu.VMEM((A_H, DP_TQ, A_DH), F32)],
        compiler_params=_cparams(("arbitrary", "arbitrary")),
        name="dsa_prompt",
    )(slab_b, slab_b, slab_f, slab_b, slab_b, slab_b, bias_tiles)


def _dsa_sample_score_kernel(pt_ref, iq_ref, g_ref, *refs, seq, n_pages):
    page_refs, o_ref, kpad = refs[:n_pages], refs[n_pages], refs[n_pages + 1]

    @pl.when(pl.program_id(0) == 0)
    def _():
        kpad[...] = jnp.zeros_like(kpad)

    row = lax.broadcasted_iota(I32, (seq, LANE), 0)
    lane = lax.broadcasted_iota(I32, (seq, LANE), 1)
    w_all = g_ref[:, SM_IW:SM_IW + IDX_H] * (IDX_D ** -0.5 * IDX_H ** -0.5)
    wb = [jnp.broadcast_to(w_all[:, h:h + 1], (seq, LANE)) for h in range(IDX_H)]
    iqs = jnp.concatenate([iq_ref[:, h * IDX_D:(h + 1) * IDX_D] for h in range(IDX_H)], axis=0).astype(BF16)

    def scores(kt):
        d = _dot_nt(iqs, kt)
        sc = jnp.zeros((seq, LANE), F32)
        for h in range(IDX_H):
            sc = sc + jnp.maximum(d[h * seq:(h + 1) * seq], 0.0) * wb[h]
        return sc

    for p in range(n_pages):
        o_ref[p] = scores(page_refs[p][...].astype(BF16))
    kpad[0:seq, :] = g_ref[:, SM_IK:SM_IK + IDX_D]
    sc = scores(kpad[...].astype(BF16))
    o_ref[n_pages] = jnp.where(lane <= row, sc, -jnp.inf)


def _page_map(p):
    return lambda b, pt: (pt[b, p], 0, 0)


def _dsa_sample_scores(page_table, slab_f, kidx_pool, seq):
    nb, n_pages = page_table.shape
    page_specs = [pl.BlockSpec((None, PAGE, IDX_D), _page_map(p)) for p in range(n_pages)]
    grid_spec = pltpu.PrefetchScalarGridSpec(
        num_scalar_prefetch=1,
        grid=(nb,),
        in_specs=[pl.BlockSpec((seq, IDX_H * IDX_D), lambda b, pt: (b, COL_IQ // 1024)),
                  pl.BlockSpec((seq, LANE), lambda b, pt: (b, COL_SM // LANE))] + page_specs,
        out_specs=pl.BlockSpec((n_pages + 1, seq, LANE), lambda b, pt: (0, b, 0)),
        scratch_shapes=[pltpu.VMEM((LANE, IDX_D), F32)],
    )
    return pl.pallas_call(
        functools.partial(_dsa_sample_score_kernel, seq=seq, n_pages=n_pages),
        grid_spec=grid_spec,
        out_shape=jax.ShapeDtypeStruct((n_pages + 1, nb * seq, LANE), F32),
        compiler_params=_cparams(("arbitrary",)),
        name="dsa_sample_scores",
    )(page_table, slab_f, slab_f, *([kidx_pool] * n_pages))


def _dsa_select_kernel(keys_ref, lo_ref, hi_ref, pos_ref, *, topk):
    lo, hi, tie_pos = _topk_select(keys_ref, keys_ref.shape[0], topk)
    lo_ref[...] = lo
    hi_ref[...] = hi
    pos_ref[...] = tie_pos


def _dsa_select(keys, topk):
    nt, rows, _ = keys.shape
    row_spec = pl.BlockSpec((LANE, LANE), lambda r: (r, 0))
    return pl.pallas_call(
        functools.partial(_dsa_select_kernel, topk=topk),
        grid=(rows // LANE,),
        in_specs=[pl.BlockSpec((nt, LANE, LANE), lambda r: (0, r, 0))],
        out_specs=[row_spec, row_spec, row_spec],
        out_shape=[jax.ShapeDtypeStruct((rows, LANE), F32),
                   jax.ShapeDtypeStruct((rows, LANE), F32),
                   jax.ShapeDtypeStruct((rows, LANE), I32)],
        compiler_params=_cparams(("arbitrary",)),
        name="dsa_select",
    )(keys)


SA_PG = 8


def _dsa_sample_attend_kernel(pt_ref, q_ref, nk_ref, nv_ref, *refs, seq, n_pages):
    kp, vp = refs[:SA_PG], refs[SA_PG:2 * SA_PG]
    keys_ref, lo_ref, hi_ref, pos_ref, bias_ref, o_ref, s_scr, vb, kpad, vpad = refs[2 * SA_PG:]
    b, g = pl.program_id(0), pl.program_id(1)
    ntile = n_pages + 1

    @pl.when((b == 0) & (g == 0))
    def _():
        kpad[...] = jnp.zeros_like(kpad)
        vpad[...] = jnp.zeros_like(vpad)

    qh = [q_ref[:, h * A_DH:(h + 1) * A_DH].astype(BF16) for h in range(A_H)]
    for j in range(SA_PG):
        t = g * SA_PG + j
        r0 = pl.multiple_of(t * PAGE, PAGE)
        for h in range(A_H):
            kh = kp[j][pl.ds(h, PAGE, stride=A_H), :].astype(BF16)
            s_scr[h, t] = _dot_nt(qh[h], kh)
            vb[h, pl.ds(r0, PAGE), :] = vp[j][pl.ds(h, PAGE, stride=A_H), :].astype(BF16)

    @pl.when(g == pl.num_programs(1) - 1)
    def _():
        kpad[0:seq, :] = nk_ref[...]
        vpad[0:seq, :] = nv_ref[...]
        lo, hi, tie_pos = lo_ref[...], hi_ref[...], pos_ref[...]
        sel = jnp.concatenate([_selected(keys_ref[t], t, lo, hi, tie_pos) for t in range(ntile)], axis=1)
        for h in range(A_H):
            sl = slice(h * A_DH, (h + 1) * A_DH)
            s_scr[h, n_pages] = _dot_nt(qh[h], kpad[:, sl].astype(BF16))
            vb[h, n_pages * PAGE:ntile * PAGE, :] = vpad[:, sl].astype(BF16)
            s = jnp.concatenate([s_scr[h, t] for t in range(ntile)], axis=1) * (A_DH ** -0.5) + bias_ref[h]
            s = jnp.where(sel, s, NEG_BIG)
            p = jnp.where(sel, jnp.exp(s - jnp.max(s, axis=-1, keepdims=True)), 0.0)
            out = _dot(p.astype(BF16), vb[h]) / jnp.sum(p, axis=-1, keepdims=True)
            o_ref[:, sl] = out.astype(o_ref.dtype)


def _group_page_map(j):
    return lambda b, g, pt: (pt[b, g * SA_PG + j], 0, 0)


def _dsa_sample_attend(page_table, slab_f, k_pool, v_pool, keys, lo, hi, tie_pos, bias_s, seq):
    nb, n_pages = page_table.shape
    assert n_pages % SA_PG == 0
    wide = A_H * A_DH
    ntile = n_pages + 1
    page_specs = [pl.BlockSpec((None, PAGE * A_H, A_DH), _group_page_map(j)) for j in range(SA_PG)]
    grid_spec = pltpu.PrefetchScalarGridSpec(
        num_scalar_prefetch=1,
        grid=(nb, n_pages // SA_PG),
        in_specs=[pl.BlockSpec((seq, wide), lambda b, g, pt: (b, COL_AQ // wide)),
                  pl.BlockSpec((seq, wide), lambda b, g, pt: (b, COL_AK // wide)),
                  pl.BlockSpec((seq, wide), lambda b, g, pt: (b, COL_AV // wide))]
                 + page_specs + page_specs
                 + [pl.BlockSpec((ntile, seq, LANE), lambda b, g, pt: (0, b, 0)),
                    pl.BlockSpec((seq, LANE), lambda b, g, pt: (b, 0)),
                    pl.BlockSpec((seq, LANE), lambda b, g, pt: (b, 0)),
                    pl.BlockSpec((seq, LANE), lambda b, g, pt: (b, 0)),
                    _resident((A_H, seq, ntile * LANE), lambda b, g, pt: (0, 0, 0))],
        out_specs=pl.BlockSpec((seq, wide), lambda b, g, pt: (b, 0)),
        scratch_shapes=[pltpu.VMEM((A_H, ntile, seq, LANE), F32),
                        pltpu.VMEM((A_H, ntile * PAGE, A_DH), BF16),
                        pltpu.VMEM((PAGE, wide), F32),
                        pltpu.VMEM((PAGE, wide), F32)],
    )
    return pl.pallas_call(
        functools.partial(_dsa_sample_attend_kernel, seq=seq, n_pages=n_pages),
        grid_spec=grid_spec,
        out_shape=jax.ShapeDtypeStruct((nb * seq, wide), F32),
        compiler_params=_cparams(("arbitrary", "arbitrary")),
        name="dsa_sample_attend",
    )(page_table, slab_f, slab_f, slab_f, *([k_pool] * SA_PG), *([v_pool] * SA_PG), keys, lo, hi, tie_pos, bias_s)


MO_TM = 256


def _mixout_kernel(mm_ref, ma_ref, x_ref, wo_ref, g_ref, wq_ref, k1_ref, k2_ref,
                   hp_ref, xnt_ref, s1_ref, s2_ref):
    half = M_H * M_DV
    y = (_dot(mm_ref[...].astype(BF16), wo_ref[0:half, :])
         + _dot(ma_ref[...].astype(BF16), wo_ref[half:, :]))
    hp = x_ref[...] + y
    hp_ref[...] = hp
    xn = hp * lax.rsqrt(jnp.mean(hp * hp, axis=-1, keepdims=True) + RMS_EPS) * g_ref[...]
    xnt = xn.T.astype(BF16)
    xnt_ref[...] = xnt
    qqt = _dot(wq_ref[...], xnt)
    hq = PEER_DQ // 2
    for h in range(PEER_H):
        q1 = qqt[h * PEER_DQ:h * PEER_DQ + hq, :].astype(BF16)
        q2 = qqt[h * PEER_DQ + hq:(h + 1) * PEER_DQ, :].astype(BF16)
        s1_ref[h] = _dot(k1_ref[h], q1)
        s2_ref[h] = _dot(k2_ref[h], q2)


def _mixout(mm, ma, x2d, wo, g2, wqt, k1, k2):
    rows = x2d.shape[0]
    half = M_H * M_DV
    c2 = lambda i: (0, 0)
    c3 = lambda i: (0, 0, 0)
    return pl.pallas_call(
        _mixout_kernel,
        grid=(rows // MO_TM,),
        in_specs=[pl.BlockSpec((MO_TM, half), lambda i: (i, 0)),
                  pl.BlockSpec((MO_TM, A_H * A_DH), lambda i: (i, 0)),
                  pl.BlockSpec((MO_TM, D_MODEL), lambda i: (i, 0)),
                  _resident((half + A_H * A_DH, D_MODEL), c2),
                  _resident((1, D_MODEL), c2),
                  _resident((PEER_H * PEER_DQ, D_MODEL), c2),
                  _resident((PEER_H, N_KEYS, PEER_DQ // 2), c3),
                  _resident((PEER_H, N_KEYS, PEER_DQ // 2), c3)],
        out_specs=[pl.BlockSpec((MO_TM, D_MODEL), lambda i: (i, 0)),
                   pl.BlockSpec((D_MODEL, MO_TM), lambda i: (0, i)),
                   pl.BlockSpec((PEER_H, N_KEYS, MO_TM), lambda i: (0, 0, i)),
                   pl.BlockSpec((PEER_H, N_KEYS, MO_TM), lambda i: (0, 0, i))],
        out_shape=[jax.ShapeDtypeStruct((rows, D_MODEL), F32),
                   jax.ShapeDtypeStruct((D_MODEL, rows), BF16),
                   jax.ShapeDtypeStruct((PEER_H, N_KEYS, rows), F32),
                   jax.ShapeDtypeStruct((PEER_H, N_KEYS, rows), F32)],
        compiler_params=_cparams(("arbitrary",)),
        name="mixout",
    )(mm, ma, x2d, wo, g2, wqt, k1, k2)


PS_TN = 256


def _extract_top(s, count):
    n = s.shape[0]
    idx = lax.broadcasted_iota(I32, s.shape, 0)
    rank = jnp.full(s.shape, float(count), F32)
    rem = s
    vals = []
    for r in range(count):
        mx = jnp.max(rem, axis=0, keepdims=True)
        first = jnp.min(jnp.where(rem == mx, idx, n), axis=0, keepdims=True)
        hit = idx == first
        rank = jnp.where(hit, float(r), rank)
        rem = jnp.where(hit, -jnp.inf, rem)
        vals.append(mx)
    return vals, rank


def _extract_top_distinct(s, count):
    rank = jnp.full(s.shape, float(count), F32)
    rem = s
    vals = []
    for r in range(count):
        mx = jnp.max(rem, axis=0, keepdims=True)
        hit = rem == mx
        rank = jnp.where(hit, float(r), rank)
        rem = jnp.where(hit, -jnp.inf, rem)
        vals.append(mx)
    marked = jnp.sum(jnp.where(rank < float(count), 1.0, 0.0), axis=0, keepdims=True)
    return vals, rank, marked == float(count)


def _extract_top_auto(s, count):
    vals, rank, ok = _extract_top_distinct(s, count)
    any_tie = jnp.min(jnp.where(ok, 1.0, 0.0)) < 0.5
    return lax.cond(any_tie, lambda: tuple(_extract_top(s, count)), lambda: (vals, rank))


_PAIR_COUNT = [PEER_TOPK // (r1 + 1) for r1 in range(PEER_TOPK)]
_PAIR_START = [sum(_PAIR_COUNT[:r1]) for r1 in range(PEER_TOPK)]
_PAIR_ROWS = -(-sum(_PAIR_COUNT) // 8) * 8


def _peer_select_kernel(s1_ref, s2_ref, li_ref, r2_ref, c1_ref, e2_ref):
    s1 = s1_ref[...]
    s2 = s2_ref[...]
    tn = s1.shape[1]
    v1, rank1 = _extract_top_auto(s1, PEER_TOPK)
    v2, rank2 = _extract_top_auto(s2, PEER_TOPK)
    k = lax.broadcasted_iota(I32, (_PAIR_ROWS, tn), 0)
    in_r1 = [(k >= _PAIR_START[r]) & (k < _PAIR_START[r] + _PAIR_COUNT[r]) for r in range(PEER_TOPK)]
    a1 = jnp.full((_PAIR_ROWS, tn), -jnp.inf, F32)
    start = jnp.zeros((_PAIR_ROWS, tn), I32)
    for r in range(PEER_TOPK):
        a1 = jnp.where(in_r1[r], v1[r], a1)
        start = jnp.where(in_r1[r], _PAIR_START[r], start)
    r2_of_row = k - start
    a2 = jnp.zeros((_PAIR_ROWS, tn), F32)
    for r in range(PEER_TOPK):
        a2 = jnp.where(r2_of_row == r, v2[r], a2)
    sc, crank = _extract_top(a1 + a2, PEER_TOPK)
    chosen = jnp.where(crank < float(PEER_TOPK), 1.0, 0.0)
    li = jnp.zeros(s1.shape, F32)
    for r in range(PEER_TOPK):
        cnt = jnp.sum(jnp.where(in_r1[r], chosen, 0.0), axis=0, keepdims=True)
        li = jnp.where(rank1 == float(r), cnt, li)
    z = jnp.zeros_like(sc[0])
    for c in range(PEER_TOPK):
        z = z + jnp.exp(sc[c] - sc[0])
    li_ref[...] = li
    c1_ref[...] = jnp.exp(s1 - v1[0]) / z
    e2 = jnp.exp(s2 - v2[0])
    for c in range(tn // LANE):
        r2_ref[c] = rank2[:, c * LANE:(c + 1) * LANE]
        e2_ref[c] = e2[:, c * LANE:(c + 1) * LANE]


def _peer_select(s1t, s2t):
    nh, nk, rows = s1t.shape
    spec = pl.BlockSpec((None, nk, PS_TN), lambda t, h: (h, 0, t))
    shp = jax.ShapeDtypeStruct((nh, nk, rows), F32)
    spec2 = pl.BlockSpec((None, PS_TN // LANE, nk, LANE), lambda t, h: (h, t, 0, 0))
    shp2 = jax.ShapeDtypeStruct((nh, rows // LANE, nk, LANE), F32)
    return pl.pallas_call(
        _peer_select_kernel,
        grid=(rows // PS_TN, nh),
        in_specs=[spec, spec],
        out_specs=[spec, spec2, spec, spec2],
        out_shape=[shp, shp2, shp, shp2],
        compiler_params=_cparams(("arbitrary", "arbitrary")),
        name="peer_select",
    )(s1t, s2t)


PD_TN = 512
PD_NI = 8
PD_EB = PD_NI * N_KEYS


def _peer_dense_kernel(xnt_ref, u_ref, v_ref, li_ref, r2_ref, c1_ref, e2_ref, hp_ref, g_ref, o_ref,
                       a0, a1, p0, p1):
    s = pl.program_id(1)

    @pl.when(s == 0)
    def _():
        o_ref[...] = jnp.zeros_like(o_ref)
        for ref in (a0, a1, p0, p1):
            ref[...] = jnp.zeros_like(ref)

    ntc = PD_TN // LANE

    def step(a_cur, a_prev, p_cur, p_prev, stages=(1, 2, 3)):
        pt = jnp.concatenate([p_cur[tc] for tc in range(ntc)], axis=1)
        ocw = D_MODEL // PD_NI
        for il in range(PD_NI):
            rows = slice(il * N_KEYS, (il + 1) * N_KEYS)
            ocols = slice(il * ocw, (il + 1) * ocw)
            if 3 in stages:
                o_ref[:, ocols] += _dot_tn(pt, v_ref[:, ocols])
            if 1 in stages:
                pre = _dot(u_ref[rows, :], xnt_ref[...])
                for tc in range(ntc):
                    a_cur[tc, rows, :] = pre[:, tc * LANE:(tc + 1) * LANE]
            if 2 not in stages:
                continue
            ng = PD_NI // (PD_NI // ntc)
            grp, tc = divmod(il, ntc)
            cols = slice(tc * LANE, (tc + 1) * LANE)
            hj = N_KEYS // 2
            for jh in range(2):
                jrows = slice(jh * hj, (jh + 1) * hj)
                w = [jnp.zeros((hj, LANE), F32) for _ in range(ng)]
                for h in range(PEER_H):
                    r2 = r2_ref[h, tc, jrows, :]
                    e2 = e2_ref[h, tc, jrows, :]
                    for q in range(ng):
                        i1 = grp * ng + q
                        li = li_ref[h, i1:i1 + 1, cols]
                        c1 = c1_ref[h, i1:i1 + 1, cols]
                        w[q] = w[q] + jnp.where(r2 < li, e2, 0.0) * c1
                for q in range(ng):
                    erows = slice((grp * ng + q) * N_KEYS + jh * hj, (grp * ng + q) * N_KEYS + (jh + 1) * hj)
                    a = a_prev[tc, erows, :]
                    act = 0.5 * a * (1.0 + lax.erf(a * (2.0 ** -0.5)))
                    p_prev[tc, erows, :] = (w[q] * act).astype(BF16)

    last = pl.num_programs(1) - 1
    inner = (s > 0) & (s < last)
    parity = lax.rem(s, 2)
    pl.when(s == 0)(lambda: step(a0, a1, p0, p1, stages=(1,)))
    pl.when(inner & (parity == 0))(lambda: step(a0, a1, p0, p1))
    pl.when(inner & (parity == 1))(lambda: step(a1, a0, p1, p0))
    pl.when(s == last)(lambda: step(a1, a0, p1, p0, stages=(3,)))

    @pl.when(s == pl.num_programs(1) - 1)
    def _():
        h = hp_ref[...] + o_ref[...]
        o_ref[...] = h * lax.rsqrt(jnp.mean(h * h, axis=-1, keepdims=True) + RMS_EPS) * g_ref[...]


def _peer_dense(xnt, u, v, li, r2, c1, e2, hp, gf):
    rows = xnt.shape[1]
    n_blocks = N_EXPERTS // PD_EB
    assert n_blocks % 2 == 0

    def blk(s, lag):
        return jnp.clip(s - lag, 0, n_blocks - 1)

    key2_spec = pl.BlockSpec((PEER_H, PD_TN // LANE, N_KEYS, LANE), lambda t, s: (0, t, 0, 0))
    key1_spec = pl.BlockSpec((PEER_H, PD_NI, PD_TN), lambda t, s: (0, blk(s, 1), t))
    return pl.pallas_call(
        _peer_dense_kernel,
        grid=(rows // PD_TN, n_blocks + 2),
        in_specs=[pl.BlockSpec((D_MODEL, PD_TN), lambda t, s: (0, t)),
                  pl.BlockSpec((PD_EB, D_MODEL), lambda t, s: (blk(s, 0), 0)),
                  pl.BlockSpec((PD_EB, D_MODEL), lambda t, s: (blk(s, 2), 0)),
                  key1_spec, key2_spec, key1_spec, key2_spec,
                  pl.BlockSpec((PD_TN, D_MODEL), lambda t, s: (t, 0)),
                  pl.BlockSpec((1, D_MODEL), lambda t, s: (0, 0))],
        out_specs=pl.BlockSpec((PD_TN, D_MODEL), lambda t, s: (t, 0)),
        out_shape=jax.ShapeDtypeStruct((rows, D_MODEL), F32),
        scratch_shapes=[pltpu.VMEM((PD_TN // LANE, PD_EB, LANE), F32), pltpu.VMEM((PD_TN // LANE, PD_EB, LANE), F32),
                        pltpu.VMEM((PD_TN // LANE, PD_EB, LANE), BF16), pltpu.VMEM((PD_TN // LANE, PD_EB, LANE), BF16)],
        compiler_params=_cparams(("arbitrary", "arbitrary")),
        name="peer_dense",
    )(xnt, u, v, li, r2, c1, e2, hp, gf)


def _t5_bucket_table(n):
    rel = np.arange(n)
    max_exact = REL_BUCKETS // 2
    nf = np.maximum(rel, 1).astype(np.float32)
    large = max_exact + (np.log(nf / np.float32(max_exact)) / np.float32(math.log(REL_MAX_DIST / max_exact))
                         * np.float32(REL_BUCKETS - max_exact)).astype(np.int32)
    large = np.minimum(large, REL_BUCKETS - 1)
    return np.where(rel < max_exact, rel, large).astype(np.int32)


def _bias_tiles(rel_bias):
    table = _t5_bucket_table(3 * LANE)
    r = np.arange(LANE)[:, None]
    c = np.arange(LANE)[None, :]
    rel = np.stack([np.maximum(r - c, 0), LANE + r - c, np.full((LANE, LANE), 2 * LANE)])
    assert table[LANE + 1] == REL_BUCKETS - 1
    buckets = table[rel]
    return _bucket_lookup(rel_bias, buckets[:, None])


def _bucket_lookup(rel_bias, buckets):
    tb = rel_bias.astype(F32)
    b = jnp.asarray(buckets)
    hshape = (1, A_H) + (1,) * (buckets.ndim - 2)
    out = jnp.zeros(buckets.shape[:1] + (A_H,) + buckets.shape[2:], F32)
    for k in range(REL_BUCKETS):
        out = jnp.where(b == k, tb[k].reshape(hshape), out)
    return out


def _sample_bias(rel_bias, past, seq, ntile):
    t = np.arange(seq)[:, None]
    c = np.arange(ntile * LANE)[None, :]
    buckets = _t5_bucket_table(past + seq)[np.maximum(past + t - c, 0)]
    return _bucket_lookup(rel_bias, buckets[None, None])[0]


def _relayout_w_in(w):
    cuts = np.cumsum(IN_SPLITS)[:-1].tolist()
    mq, mk, mv, mo, ig, fg, aq, ak, av, iq, ik, iw = jnp.split(w, cuts, axis=-1)
    pad = jnp.zeros((w.shape[0], P_PAD - COL_SM - IDX_D - IDX_H - 2 * M_H), w.dtype)
    return jnp.concatenate([aq, ak, av, iq, mv, mo, mq, mk, ik, iw, ig, fg, pad], axis=-1).astype(BF16)


def kernel(x_prompt, x_sample, cache_k, cache_v, cache_kidx, state_C, state_n, state_m, page_table,
           meta_tokens, rel_bias, final_norm_g, norm1_g, w_in, b_i, b_f, mh_norm_g, w_out, norm2_g,
           peer_wq, peer_k1, peer_k2, peer_u, peer_v):
    B, S_p, D = x_prompt.shape
    Bs, Ts, _ = x_sample.shape
    n_pages = page_table.shape[1]
    depth = w_in.shape[0]
    assert depth == 1 and D == D_MODEL and N_META + S_p == T_PAD - META_OFF and S_p % LANE == 0
    assert n_pages * PAGE + Ts <= (n_pages + 1) * PAGE and cache_k.shape[2] == PAGE
    topk_p = min(TOPK_MAX, S_p // 4)
    topk_s = min(TOPK_MAX, (n_pages * PAGE + Ts) // 4)
    l = 0

    w1 = _relayout_w_in(w_in[l])
    g1 = norm1_g[l].astype(F32).reshape(1, D)
    g2 = norm2_g[l].astype(F32).reshape(1, D)
    gf = final_norm_g.astype(F32).reshape(1, D)
    zpad = jnp.zeros((LANE - SM_FG - M_H,), F32)
    bias_row = jnp.concatenate([jnp.zeros((SM_IG,), F32), b_i[l].astype(F32), b_f[l].astype(F32), zpad]).reshape(1, LANE)
    mhg = mh_norm_g[l].astype(F32).reshape(1, M_H * M_DV)
    wo = w_out[l].astype(BF16)
    wqt = peer_wq[l].T.astype(BF16)
    k1 = peer_k1[l].astype(BF16)
    k2 = peer_k2[l].astype(BF16)
    u = peer_u[l].astype(BF16)
    v = peer_v[l].astype(BF16)
    bias_tiles = _bias_tiles(rel_bias)

    meta = jnp.broadcast_to(meta_tokens.astype(F32)[None], (B, N_META, D))
    xp_all = jnp.concatenate([x_prompt, jnp.zeros((B, META_OFF, D), F32), meta], axis=1)

    pf, pb = _inproj(xp_all.reshape(B * T_PAD, D), g1, w1)
    pf = pf.reshape(B, T_PAD, P_PAD)
    pb = pb.reshape(B, T_PAD, P_PAD)
    mm_p, c_p, n_p, m_p = _mlstm_prompt(pb, pf, bias_row, mhg)
    ma_p = _dsa_prompt(pb, pf, bias_tiles, topk_p)
    xp2 = x_prompt.reshape(B * S_p, D)
    hp_p, xnt_p, s1_p, s2_p = _mixout(mm_p.reshape(B * S_p, -1), ma_p.reshape(B * S_p, -1), xp2, wo, g2, wqt, k1, k2)
    y_prompt = _peer_dense(xnt_p, u, v, *_peer_select(s1_p, s2_p), hp_p, gf).reshape(B, S_p, D)

    xs2 = x_sample.reshape(Bs * Ts, D)
    sf, _ = _inproj(xs2, g1, w1)
    c0 = state_C[l].astype(F32)
    n0 = state_n[l].astype(F32).reshape(Bs, M_H, 1, M_DK)
    m0 = jnp.broadcast_to(state_m[l].astype(F32)[:, :, None, None], (Bs, M_H, 1, LANE))
    mm_s, c_s, n_s, m_s = _mlstm_sample(sf, bias_row, mhg, c0, n0, m0, Ts)
    n_pool = cache_k.shape[1]
    keys_s = _dsa_sample_scores(page_table, sf, cache_kidx[l].reshape(n_pool, PAGE, IDX_D), Ts)
    lo_s, hi_s, pos_s = _dsa_select(keys_s, topk_s)
    bias_s = _sample_bias(rel_bias, n_pages * PAGE, Ts, n_pages + 1)
    ma_s = _dsa_sample_attend(page_table, sf, cache_k[l].reshape(n_pool, PAGE * A_H, A_DH),
                              cache_v[l].reshape(n_pool, PAGE * A_H, A_DH), keys_s, lo_s, hi_s, pos_s, bias_s, Ts)
    hp_s, xnt_s, s1_s, s2_s = _mixout(mm_s, ma_s, xs2, wo, g2, wqt, k1, k2)
    y_sample = _peer_dense(xnt_s, u, v, *_peer_select(s1_s, s2_s), hp_s, gf).reshape(Bs, Ts, D)

    def seq_rows(a):
        return jnp.concatenate([a[:, T_PAD - N_META:], a[:, :S_p]], axis=1)

    wide = A_H * A_DH
    T = N_META + S_p
    k_prompt = seq_rows(pf[:, :, COL_AK:COL_AK + wide]).reshape(1, B, T, A_H, A_DH)
    v_prompt = seq_rows(pf[:, :, COL_AV:COL_AV + wide]).reshape(1, B, T, A_H, A_DH)
    kidx_prompt = seq_rows(pf[:, :, COL_SM + SM_IK:COL_SM + SM_IK + IDX_D]).reshape(1, B, T, IDX_D)
    k_sample = sf[:, COL_AK:COL_AK + wide].reshape(1, Bs, Ts, A_H, A_DH)
    v_sample = sf[:, COL_AV:COL_AV + wide].reshape(1, Bs, Ts, A_H, A_DH)
    kidx_sample = sf[:, COL_SM + SM_IK:COL_SM + SM_IK + IDX_D].reshape(1, Bs, Ts, IDX_D)
    return (y_prompt, y_sample,
            k_prompt, v_prompt, kidx_prompt,
            c_p[None], n_p[:, :, 0, :][None], m_p[:, :, 0, 0][None],
            k_sample, v_sample, kidx_sample,
            c_s[None], n_s[:, :, 0, :][None], m_s[:, :, 0, 0][None])
```
